```python
import math
import jax, jax.numpy as jnp
from jax import lax
import numpy as np

D_MODEL = 2048
BATCH = 2
SEQ = 16384
DEPTH = 4

D_MIX = D_MODEL
ATTN_WIDTH = D_MIX // 2
ATTN_HD = 64
N_Q_HEADS = ATTN_WIDTH // ATTN_HD
N_KV_HEADS = N_Q_HEADS // 8
GQA_GROUP = N_Q_HEADS // N_KV_HEADS
WINDOW = 128
BLOCK = 128
GLA_WIDTH = D_MIX - ATTN_WIDTH
GLA_HEADS = 4
GLA_DV = GLA_WIDTH // GLA_HEADS
GLA_DK = GLA_DV // 2
GLA_RANK = 16
GLA_TAU = 16.0
GLA_CHUNK = 64
D_FF = ((8 * D_MODEL // 3 + 255) // 256) * 256
N_MOD = 9
EPS = 1e-6

kernel_name = "hybrid_swa_sink_alibi_gla_macaron_adaln"


def _split_sizes():
    return [N_Q_HEADS * ATTN_HD, N_KV_HEADS * ATTN_HD, N_KV_HEADS * ATTN_HD,
            GLA_HEADS * GLA_DK, GLA_HEADS * GLA_DK, GLA_HEADS * GLA_DV,
            GLA_HEADS * GLA_DV, GLA_RANK]


def _rmsnorm_mod(x, w, shift, scale):
    xf = x.astype(jnp.float32)
    y = xf * lax.rsqrt(jnp.mean(xf * xf, axis=-1, keepdims=True) + EPS) * w.astype(jnp.float32)
    y = y * (1.0 + scale.astype(jnp.float32)) + shift.astype(jnp.float32)
    return y.astype(x.dtype)


def _swiglu(h, w1, w3, w2):
    return (jax.nn.silu(h @ w1) * (h @ w3)) @ w2


def _alibi_slopes(n):
    return jnp.exp2(-8.0 * jnp.arange(1, n + 1, dtype=jnp.float32) / n)


def _swa_sink_attention(q, k, v, sinks):
    B, S, _ = q.shape
    nb = S // BLOCK
    q = q.reshape(B, nb, BLOCK, N_KV_HEADS, GQA_GROUP, ATTN_HD)

    def slab(t):
        t = t.reshape(B, S, N_KV_HEADS, ATTN_HD)
        t = jnp.pad(t, ((0, 0), (BLOCK, 0), (0, 0), (0, 0)))
        t = t.reshape(B, nb + 1, BLOCK, N_KV_HEADS, ATTN_HD)
        return jnp.concatenate([t[:, :-1], t[:, 1:]], axis=2)

    ks, vs = slab(k), slab(v)
    s = jnp.einsum('bnqkgd,bnskd->bnkgqs', q, ks).astype(jnp.float32) * (ATTN_HD ** -0.5)
    qi = jnp.arange(BLOCK)[:, None]
    kj = jnp.arange(2 * BLOCK)[None, :]
    dist = qi + BLOCK - kj
    band = (dist >= 0) & (dist < WINDOW)
    valid = band[None] & ((jnp.arange(nb) > 0)[:, None, None] | (kj >= BLOCK)[None])
    slopes = _alibi_slopes(N_Q_HEADS).reshape(N_KV_HEADS, GQA_GROUP)
    bias = -slopes[:, :, None, None] * dist.astype(jnp.float32)[None, None]
    s = jnp.where(valid[None, :, None, None], s + bias[None, None], -jnp.inf)
    sink = sinks.astype(jnp.float32).reshape(N_KV_HEADS, GQA_GROUP)[None, None, :, :, None, None]
    m = jnp.maximum(jnp.max(s, axis=-1, keepdims=True), sink)
    p = jnp.exp(s - m)
    p = p / (jnp.sum(p, axis=-1, keepdims=True) + jnp.exp(sink - m))
    o = jnp.einsum('bnkgqs,bnskd->bnqkgd', p.astype(vs.dtype), vs)
    return o.reshape(B, S, N_Q_HEADS * ATTN_HD)


def _gla(q, k, v, g, r, norm_w):
    B, S, _ = q.shape
    N = S // GLA_CHUNK
    out_dtype = v.dtype

    def chunks(t, d):
        return t.reshape(B, N, GLA_CHUNK, GLA_HEADS, d).transpose(0, 3, 1, 2, 4).astype(jnp.float32)

    q = chunks(q, GLA_DK) * (GLA_DK ** -0.5)
    k = chunks(k, GLA_DK)
    v = chunks(v, GLA_DV)
    b = jnp.cumsum(chunks(g, GLA_DK), axis=3)
    q_t = q * jnp.exp(b)
    k_t = k * jnp.exp(-b)
    causal = jnp.tril(jnp.ones((GLA_CHUNK, GLA_CHUNK), dtype=bool))
    a = jnp.where(causal, jnp.einsum('bhncd,bhnsd->bhncs', q_t, k_t), 0.0)
    o_intra = jnp.einsum('bhncs,bhnse->bhnce', a, v)
    b_last = b[:, :, :, -1:, :]
    k_end = k * jnp.exp(b_last - b)
    decay = jnp.exp(b_last[:, :, :, 0, :])

    def step(state, inp):
        qc, kc, vc, dc = inp
        o = jnp.einsum('bhcd,bhde->bhce', qc, state)
        state = dc[..., None] * state + jnp.einsum('bhcd,bhce->bhde', kc, vc)
        return state, o

    xs = (jnp.moveaxis(q_t, 2, 0), jnp.moveaxis(k_end, 2, 0), jnp.moveaxis(v, 2, 0), jnp.moveaxis(decay, 2, 0))
    s0 = jnp.zeros((B, GLA_HEADS, GLA_DK, GLA_DV), jnp.float32)
    _, o_inter = lax.scan(step, s0, xs)
    o = o_intra + jnp.moveaxis(o_inter, 0, 2)
    o = o.transpose(0, 2, 3, 1, 4).reshape(B, S, GLA_HEADS, GLA_DV)
    o = o * lax.rsqrt(jnp.mean(o * o, axis=-1, keepdims=True) + EPS) * norm_w.astype(jnp.float32)
    o = o.reshape(B, S, GLA_HEADS * GLA_DV) * jax.nn.silu(r.astype(jnp.float32))
    return o.astype(out_dtype)


def _mixer(h, w_in, gate_w2, gate_b, sinks, gla_norm, w_out):
    p = h @ w_in
    idx = list(np.cumsum(_split_sizes())[:-1])
    q_a, k_a, v_a, q_g, k_g, v_g, r_g, gate_lr = jnp.split(p, idx, axis=-1)
    g = jax.nn.log_sigmoid((gate_lr @ gate_w2 + gate_b).astype(jnp.float32)) / GLA_TAU
    o_attn = _swa_sink_attention(q_a, k_a, v_a, sinks)
    o_gla = _gla(q_g, k_g, v_g, g, r_g, gla_norm)
    return jnp.concatenate([o_attn, o_gla.astype(o_attn.dtype)], axis=-1) @ w_out


def setup_inputs(seed: int = 0) -> dict:
    key = jax.random.key(seed)
    ks = jax.random.split(key, 24)
    nrm = jax.random.normal
    f32 = jnp.float32
    n_in = sum(_split_sizes())
    L, D = DEPTH, D_MODEL
    return {
        "x": nrm(ks[0], (BATCH, SEQ, D), f32),
        "c": nrm(ks[1], (BATCH, D), f32),
        "ada_w": nrm(ks[2], (L, D, N_MOD * D), f32) * D ** -0.5,
        "ada_b": nrm(ks[3], (L, N_MOD * D), f32) * 0.01,
        "norm_ffn1": 1.0 + 0.05 * nrm(ks[4], (L, D), f32),
        "ffn1_w1": nrm(ks[5], (L, D, D_FF), f32) * D ** -0.5,
        "ffn1_w3": nrm(ks[6], (L, D, D_FF), f32) * D ** -0.5,
        "ffn1_w2": nrm(ks[7], (L, D_FF, D), f32) * D_FF ** -0.5,
        "norm_mix": 1.0 + 0.05 * nrm(ks[8], (L, D), f32),
        "w_in": nrm(ks[9], (L, D, n_in), f32) * D ** -0.5,
        "gla_gate_w2": nrm(ks[10], (L, GLA_RANK, GLA_HEADS * GLA_DK), f32) * GLA_RANK ** -0.5,
        "gla_gate_b": nrm(ks[11], (L, GLA_HEADS * GLA_DK), f32) * 0.01,
        "attn_sinks": nrm(ks[12], (L, N_Q_HEADS), f32) * 0.5,
        "gla_norm": 1.0 + 0.05 * nrm(ks[13], (L, GLA_DV), f32),
        "w_out": nrm(ks[14], (L, D_MIX, D), f32) * D_MIX ** -0.5,
        "norm_ffn2": 1.0 + 0.05 * nrm(ks[15], (L, D), f32),
        "ffn2_w1": nrm(ks[16], (L, D, D_FF), f32) * D ** -0.5,
        "ffn2_w3": nrm(ks[17], (L, D, D_FF), f32) * D ** -0.5,
        "ffn2_w2": nrm(ks[18], (L, D_FF, D), f32) * D_FF ** -0.5,
        "final_ada_w": nrm(ks[19], (D, 2 * D), f32) * D ** -0.5,
        "final_ada_b": nrm(ks[20], (2 * D,), f32) * 0.01,
        "final_norm": 1.0 + 0.05 * nrm(ks[21], (D,), f32),
    }


def reference(x, c, ada_w, ada_b, norm_ffn1, ffn1_w1, ffn1_w3, ffn1_w2, norm_mix, w_in,
              gla_gate_w2, gla_gate_b, attn_sinks, gla_norm, w_out, norm_ffn2, ffn2_w1,
              ffn2_w3, ffn2_w2, final_ada_w, final_ada_b, final_norm):
    B = x.shape[0]
    c_act = jax.nn.silu(c)
    h = x
    for l in range(DEPTH):
        mod = (c_act @ ada_w[l] + ada_b[l]).reshape(B, N_MOD, D_MODEL)[:, :, None, :]
        sh1, sc1, g1, sh2, sc2, g2, sh3, sc3, g3 = [mod[:, i] for i in range(N_MOD)]
        u = _rmsnorm_mod(h, norm_ffn1[l], sh1, sc1)
        h = h + 0.5 * g1 * _swiglu(u, ffn1_w1[l], ffn1_w3[l], ffn1_w2[l])
        u = _rmsnorm_mod(h, norm_mix[l], sh2, sc2)
        h = h + g2 * _mixer(u, w_in[l], gla_gate_w2[l], gla_gate_b[l], attn_sinks[l], gla_norm[l], w_out[l])
        u = _rmsnorm_mod(h, norm_ffn2[l], sh3, sc3)
        h = h + 0.5 * g3 * _swiglu(u, ffn2_w1[l], ffn2_w3[l], ffn2_w2[l])
    fmod = (c_act @ final_ada_w + final_ada_b).reshape(B, 2, D_MODEL)[:, :, None, :]
    return _rmsnorm_mod(h, final_norm, fmod[:, 0], fmod[:, 1])
```

```python
import functools

import jax
import jax.numpy as jnp
import numpy as np
from jax import lax
from jax.experimental import pallas as pl
from jax.experimental.pallas import tpu as pltpu

F32 = jnp.float32
BF16 = jnp.bfloat16

ATTN_HD = 64
N_Q_HEADS = 16
N_KV_HEADS = 2
GQA_GROUP = N_Q_HEADS // N_KV_HEADS
PAIRS_PER_KV = GQA_GROUP // 2
WINDOW = 128
BLOCK = 128
GLA_HEADS = 4
GLA_DK = 128
GLA_DV = 256
GLA_RANK = 16
GLA_TAU = 16.0
GLA_CHUNK = 64
N_MOD = 9
EPS = 1e-6
MASK_VALUE = -1e30

LANES = 128
SUBLANES = 8
VMEM_LIMIT_BYTES = 56 * 1024 * 1024

FFN_TM = 512
FFN_TF = 512
PROJ_TM = 256
ATTN_TQ = 512
GLA_TT = 256
OUT_TM = 512
FIN_TM = 512
MOD_TN = 1024

ATTN_W = N_Q_HEADS * ATTN_HD
KV_DUP_W = N_KV_HEADS * 2 * ATTN_HD
GK_W = GLA_HEADS * GLA_DK
GV_W = GLA_HEADS * GLA_DV
_SEG = {}
_off = 0
for _name, _w in (("qa", ATTN_W), ("kd", KV_DUP_W), ("vd", KV_DUP_W), ("qg", GK_W), ("kg", GK_W),
                  ("vg", GV_W), ("rg", GV_W), ("gate", LANES)):
    _SEG[_name] = (_off, _w)
    _off += _w
PROJ_W = _off


def _params(*sem):
    return pltpu.CompilerParams(dimension_semantics=sem, vmem_limit_bytes=VMEM_LIMIT_BYTES)


def _rms_mod(x, nw, scale, shift):
    y = x * lax.rsqrt(jnp.mean(x * x, axis=-1, keepdims=True) + EPS) * nw
    return y * (1.0 + scale) + shift


def _silu(x):
    return x * jax.nn.sigmoid(x)


def _mod_kernel(c_ref, w_ref, b_ref, o_ref):
    ca = _silu(c_ref[...]).astype(BF16)
    o_ref[...] = jnp.dot(ca, w_ref[...].astype(BF16), preferred_element_type=F32) + b_ref[...]


def _modulation(c_pad, w, b):
    L, D, N = w.shape
    rows = c_pad.shape[0]
    return pl.pallas_call(
        _mod_kernel,
        grid=(L, N // MOD_TN),
        in_specs=[
            pl.BlockSpec((rows, D), lambda l, j: (0, 0)),
            pl.BlockSpec((None, D, MOD_TN), lambda l, j: (l, 0, j)),
            pl.BlockSpec((None, 1, MOD_TN), lambda l, j: (l, 0, j)),
        ],
        out_specs=pl.BlockSpec((None, rows, MOD_TN), lambda l, j: (l, 0, j)),
        out_shape=jax.ShapeDtypeStruct((L, rows, N), F32),
        compiler_params=_params("parallel", "parallel"),
        name="adaln_modulation",
    )(c_pad, w, b.reshape(L, 1, N))


def _ffn_kernel(h_ref, mod_ref, nw_ref, w1_ref, w3_ref, w2_ref, o_ref, u_ref, *, row0, nf):
    f = pl.program_id(1)

    @pl.when(f == 0)
    def _():
        u = _rms_mod(h_ref[...], nw_ref[...], mod_ref[row0 + 1:row0 + 2, :], mod_ref[row0:row0 + 1, :])
        u_ref[...] = u.astype(BF16)
        o_ref[...] = jnp.zeros_like(o_ref)

    u = u_ref[...]
    a = jnp.dot(u, w1_ref[...], preferred_element_type=F32)
    b = jnp.dot(u, w3_ref[...], preferred_element_type=F32)
    hm = (_silu(a) * b).astype(BF16)
    o_ref[...] += jnp.dot(hm, w2_ref[...], preferred_element_type=F32)

    @pl.when(f == nf - 1)
    def _():
        o_ref[...] = h_ref[...] + (0.5 * mod_ref[row0 + 2:row0 + 3, :]) * o_ref[...]


def _ffn(h, modl, nw, w1, w3, w2, l, row0, seq):
    T, D = h.shape
    F = w1.shape[2]
    nf = F // FFN_TF
    tpb = seq // FFN_TM
    return pl.pallas_call(
        functools.partial(_ffn_kernel, row0=row0, nf=nf),
        grid=(T // FFN_TM, nf),
        in_specs=[
            pl.BlockSpec((FFN_TM, D), lambda i, f: (i, 0)),
            pl.BlockSpec((None, N_MOD, D), lambda i, f: (i // tpb, 0, 0)),
            pl.BlockSpec((None, 1, D), lambda i, f: (l, 0, 0)),
            pl.BlockSpec((None, D, FFN_TF), lambda i, f: (l, 0, f)),
            pl.BlockSpec((None, D, FFN_TF), lambda i, f: (l, 0, f)),
            pl.BlockSpec((None, FFN_TF, D), lambda i, f: (l, f, 0)),
        ],
        out_specs=pl.BlockSpec((FFN_TM, D), lambda i, f: (i, 0)),
        out_shape=jax.ShapeDtypeStruct((T, D), F32),
        scratch_shapes=[pltpu.VMEM((FFN_TM, D), BF16)],
        compiler_params=_params("parallel", "arbitrary"),
        name="swiglu_ffn",
    )(h, modl, nw, w1, w3, w2)


def _log_sigmoid(x):
    return jnp.minimum(x, 0.0) - jnp.log1p(jnp.exp(-jnp.abs(x)))


def _proj_kernel(h_ref, mod_ref, nw_ref, w_ref, gw_ref, gb_ref,
                 qa_ref, kd_ref, vd_ref, qg_ref, kg_ref, vg_ref, rg_ref, gg_ref):
    u = _rms_mod(h_ref[...], nw_ref[...], mod_ref[4:5, :], mod_ref[3:4, :]).astype(BF16)

    def seg(name):
        s, w = _SEG[name]
        return jnp.dot(u, w_ref[:, s:s + w], preferred_element_type=F32)

    qa_ref[...] = seg("qa").astype(BF16)
    kd_ref[...] = seg("kd").astype(BF16)
    vd_ref[...] = seg("vd").astype(BF16)
    qg_ref[...] = seg("qg")
    kg_ref[...] = seg("kg")
    vg_ref[...] = seg("vg").astype(BF16)
    rg_ref[...] = seg("rg")
    gate_lr = seg("gate").astype(BF16)
    z = jnp.dot(gate_lr, gw_ref[...], preferred_element_type=F32) + gb_ref[...]
    gg_ref[...] = _log_sigmoid(z) / GLA_TAU


def _proj(h, modl, nw, w_in_p, gate_w2_p, gate_b, l, seq):
    T, D = h.shape
    tpb = seq // PROJ_TM
    widths = (("qa", BF16), ("kd", BF16), ("vd", BF16), ("qg", F32), ("kg", F32), ("vg", BF16),
              ("rg", F32))
    out_shapes = [jax.ShapeDtypeStruct((T, _SEG[n][1]), dt) for n, dt in widths]
    out_shapes.append(jax.ShapeDtypeStruct((T, GK_W), F32))
    out_specs = [pl.BlockSpec((PROJ_TM, s.shape[1]), lambda i: (i, 0)) for s in out_shapes]
    return pl.pallas_call(
        _proj_kernel,
        grid=(T // PROJ_TM,),
        in_specs=[
            pl.BlockSpec((PROJ_TM, D), lambda i: (i, 0)),
            pl.BlockSpec((None, N_MOD, D), lambda i: (i // tpb, 0, 0)),
            pl.BlockSpec((None, 1, D), lambda i: (l, 0, 0)),
            pl.BlockSpec((None, D, PROJ_W), lambda i: (l, 0, 0)),
            pl.BlockSpec((None, LANES, GK_W), lambda i: (l, 0, 0)),
            pl.BlockSpec((None, 1, GK_W), lambda i: (l, 0, 0)),
        ],
        out_specs=out_specs,
        out_shape=out_shapes,
        compiler_params=_params("parallel"),
        name="mixer_in_proj",
    )(h, modl, nw, w_in_p, gate_w2_p, gate_b)


def _attn_bias_table():
    qi = np.arange(BLOCK)[:, None]
    kj = np.arange(2 * BLOCK)[None, :]
    dist = qi + BLOCK - kj
    band = (dist >= 0) & (dist < WINDOW)
    slopes = np.exp2(-8.0 * np.arange(1, N_Q_HEADS + 1, dtype=np.float32) / N_Q_HEADS).astype(np.float32)
    tab = np.empty((2, N_KV_HEADS, PAIRS_PER_KV * BLOCK, 4 * BLOCK), np.float32)
    for var in range(2):
        valid = band & (kj >= BLOCK) if var else band
        for kh in range(N_KV_HEADS):
            for p in range(PAIRS_PER_KV):
                for e in range(2):
                    hd = kh * GQA_GROUP + 2 * p + e
                    bias = np.where(valid, -slopes[hd] * dist.astype(np.float32), np.float32(MASK_VALUE))
                    tab[var, kh, p * BLOCK:(p + 1) * BLOCK, e * 2 * BLOCK:(e + 1) * 2 * BLOCK] = bias
    return tab


def _attn_kernel(sink_ref, q_ref, kc_ref, vc_ref, kp_ref, vp_ref, bias_ref, o_ref):
    i = pl.program_id(1)
    first = jnp.where(i == 0, 1, 0)
    lane = lax.broadcasted_iota(jnp.int32, (2 * BLOCK, LANES), 1)
    lo = lane < ATTN_HD
    nb = ATTN_TQ // BLOCK
    for j in range(nb):
        rows = slice(j * BLOCK, (j + 1) * BLOCK)
        for kh in range(N_KV_HEADS):
            cols = slice(kh * LANES, (kh + 1) * LANES)
            if j == 0:
                kprev, vprev = kp_ref[:, cols], vp_ref[:, cols]
            else:
                prows = slice((j - 1) * BLOCK, j * BLOCK)
                kprev, vprev = kc_ref[prows, cols], vc_ref[prows, cols]
            kk = jnp.concatenate([kprev, kc_ref[rows, cols]], axis=0)
            vv = jnp.concatenate([vprev, vc_ref[rows, cols]], axis=0)
            zero = jnp.zeros_like(kk)
            kbd = jnp.concatenate([jnp.where(lo, kk, zero), jnp.where(lo, zero, kk)], axis=0)
            vbd = jnp.concatenate([jnp.where(lo, vv, zero), jnp.where(lo, zero, vv)], axis=0)
            qs = jnp.concatenate(
                [q_ref[rows, (kh * PAIRS_PER_KV + p) * LANES:(kh * PAIRS_PER_KV + p + 1) * LANES]
                 for p in range(PAIRS_PER_KV)], axis=0)
            s = lax.dot_general(qs, kbd, (((1,), (1,)), ((), ())), preferred_element_type=F32)
            bias = bias_ref[first, kh] if j == 0 else bias_ref[0, kh]
            s = s * (ATTN_HD ** -0.5) + bias
            prow = []
            for p in range(PAIRS_PER_KV):
                pe = []
                for e in range(2):
                    sink = sink_ref[kh * GQA_GROUP + 2 * p + e]
                    t = s[p * BLOCK:(p + 1) * BLOCK, e * 2 * BLOCK:(e + 1) * 2 * BLOCK]
                    m = jnp.maximum(jnp.max(t, axis=-1, keepdims=True), sink)
                    pexp = jnp.exp(t - m)
                    den = jnp.sum(pexp, axis=-1, keepdims=True) + jnp.exp(sink - m)
                    pe.append((pexp / den).astype(BF16))
                prow.append(jnp.concatenate(pe, axis=1))
            pmat = jnp.concatenate(prow, axis=0)
            o = jnp.dot(pmat, vbd, preferred_element_type=F32)
            for p in range(PAIRS_PER_KV):
                c0 = (kh * PAIRS_PER_KV + p) * LANES
                o_ref[rows, c0:c0 + LANES] = o[p * BLOCK:(p + 1) * BLOCK].astype(BF16)


def _attention(qa, kd, vd, sinks, bias_tab, batch, seq):
    nt = seq // ATTN_TQ
    bpt = ATTN_TQ // BLOCK
    cur = lambda b, i: (b, i, 0)
    prev = lambda b, i: (b, jnp.maximum(i * bpt - 1, 0), 0)
    return pl.pallas_call(
        _attn_kernel,
        grid=(batch, nt),
        in_specs=[
            pl.BlockSpec(memory_space=pltpu.SMEM),
            pl.BlockSpec((None, ATTN_TQ, ATTN_W), cur),
            pl.BlockSpec((None, ATTN_TQ, KV_DUP_W), cur),
            pl.BlockSpec((None, ATTN_TQ, KV_DUP_W), cur),
            pl.BlockSpec((None, BLOCK, KV_DUP_W), prev),
            pl.BlockSpec((None, BLOCK, KV_DUP_W), prev),
            pl.BlockSpec(bias_tab.shape, lambda b, i: (0, 0, 0, 0)),
        ],
        out_specs=pl.BlockSpec((None, ATTN_TQ, ATTN_W), cur),
        out_shape=jax.ShapeDtypeStruct((batch, seq, ATTN_W), BF16),
        compiler_params=_params("parallel", "parallel"),
        name="swa_attention",
    )(sinks, qa, kd, vd, kd, vd, bias_tab)


def _gla_kernel(q_ref, k_ref, v_ref, g_ref, r_ref, nw_ref, o_ref, st_ref):
    @pl.when(pl.program_id(1) == 0)
    def _():
        st_ref[...] = jnp.zeros_like(st_ref)

    ci = lax.broadcasted_iota(jnp.int32, (GLA_CHUNK, GLA_CHUNK), 0)
    si = lax.broadcasted_iota(jnp.int32, (GLA_CHUNK, GLA_CHUNK), 1)
    causal = si <= ci
    tri = causal.astype(BF16)
    nw = nw_ref[...]
    for c in range(GLA_TT // GLA_CHUNK):
        rows = slice(c * GLA_CHUNK, (c + 1) * GLA_CHUNK)
        g = g_ref[rows, :]
        g_hi = g.astype(BF16)
        g_r1 = g - g_hi.astype(F32)
        g_mid = g_r1.astype(BF16)
        g_lo = (g_r1 - g_mid.astype(F32)).astype(BF16)
        b = (jnp.dot(tri, g_hi, preferred_element_type=F32)
             + jnp.dot(tri, g_mid, preferred_element_type=F32)
             + jnp.dot(tri, g_lo, preferred_element_type=F32))
        b_last = b[GLA_CHUNK - 1:GLA_CHUNK, :]
        q = q_ref[rows, :] * (GLA_DK ** -0.5)
        k = k_ref[rows, :]
        q_t = (q * jnp.exp(b)).astype(BF16)
        k_t = (k * jnp.exp(-b)).astype(BF16)
        k_end = (k * jnp.exp(b_last - b)).astype(BF16)
        decay = jnp.exp(b_last)
        for hd in range(GLA_HEADS):
            kc = slice(hd * GLA_DK, (hd + 1) * GLA_DK)
            vc = slice(hd * GLA_DV, (hd + 1) * GLA_DV)
            v = v_ref[rows, vc]
            a = lax.dot_general(q_t[:, kc], k_t[:, kc], (((1,), (1,)), ((), ())),
                                preferred_element_type=F32)
            a = jnp.where(causal, a, 0.0).astype(BF16)
            st = st_ref[hd]
            o = jnp.dot(a, v, preferred_element_type=F32)
            o = o + lax.dot_general(q_t[:, kc], st.astype(BF16), (((1,), (1,)), ((), ())),
                                    preferred_element_type=F32)
            upd = lax.dot_general(v, k_end[:, kc], (((0,), (0,)), ((), ())),
                                  preferred_element_type=F32)
            st_ref[hd] = st * decay[:, kc] + upd
            o = o * lax.rsqrt(jnp.mean(o * o, axis=-1, keepdims=True) + EPS) * nw
            o_ref[rows, vc] = (o * _silu(r_ref[rows, vc])).astype(BF16)


def _gla(qg, kg, vg, gg, rg, nw, l, batch, seq):
    tile = lambda b, i: (b, i, 0)
    return pl.pallas_call(
        _gla_kernel,
        grid=(batch, seq // GLA_TT),
        in_specs=[
            pl.BlockSpec((None, GLA_TT, GK_W), tile),
            pl.BlockSpec((None, GLA_TT, GK_W), tile),
            pl.BlockSpec((None, GLA_TT, GV_W), tile),
            pl.BlockSpec((None, GLA_TT, GK_W), tile),
            pl.BlockSpec((None, GLA_TT, GV_W), tile),
            pl.BlockSpec((None, 1, GLA_DV), lambda b, i: (l, 0, 0)),
        ],
        out_specs=pl.BlockSpec((None, GLA_TT, GV_W), tile),
        out_shape=jax.ShapeDtypeStruct((batch, seq, GV_W), BF16),
        scratch_shapes=[pltpu.VMEM((GLA_HEADS, GLA_DV, GLA_DK), F32)],
        compiler_params=_params("parallel", "arbitrary"),
        name="gla_chunked",
    )(qg, kg, vg, gg, rg, nw)


def _out_kernel(h_ref, mod_ref, oa_ref, og_ref, w_ref, o_ref):
    ka = oa_ref.shape[1]
    y = jnp.dot(oa_ref[...], w_ref[:ka, :], preferred_element_type=F32)
    y = y + jnp.dot(og_ref[...], w_ref[ka:, :], preferred_element_type=F32)
    o_ref[...] = h_ref[...] + mod_ref[5:6, :] * y


def _out_proj(h, modl, oa, og, w_out, l, seq):
    T, D = h.shape
    tpb = seq // OUT_TM
    return pl.pallas_call(
        _out_kernel,
        grid=(T // OUT_TM,),
        in_specs=[
            pl.BlockSpec((OUT_TM, D), lambda i: (i, 0)),
            pl.BlockSpec((None, N_MOD, D), lambda i: (i // tpb, 0, 0)),
            pl.BlockSpec((OUT_TM, oa.shape[1]), lambda i: (i, 0)),
            pl.BlockSpec((OUT_TM, og.shape[1]), lambda i: (i, 0)),
            pl.BlockSpec((None,) + w_out.shape[1:], lambda i: (l, 0, 0)),
        ],
        out_specs=pl.BlockSpec((OUT_TM, D), lambda i: (i, 0)),
        out_shape=jax.ShapeDtypeStruct((T, D), F32),
        compiler_params=_params("parallel"),
        name="mixer_out_proj",
    )(h, modl, oa, og, w_out)


def _final_kernel(h_ref, mod_ref, nw_ref, o_ref):
    o_ref[...] = _rms_mod(h_ref[...], nw_ref[...], mod_ref[1:2, :], mod_ref[0:1, :])


def _final_norm(h, fmod, nw, seq):
    T, D = h.shape
    tpb = seq // FIN_TM
    return pl.pallas_call(
        _final_kernel,
        grid=(T // FIN_TM,),
        in_specs=[
            pl.BlockSpec((FIN_TM, D), lambda i: (i, 0)),
            pl.BlockSpec((None, 2, D), lambda i: (i // tpb, 0, 0)),
            pl.BlockSpec((1, D), lambda i: (0, 0)),
        ],
        out_specs=pl.BlockSpec((FIN_TM, D), lambda i: (i, 0)),
        out_shape=jax.ShapeDtypeStruct((T, D), F32),
        compiler_params=_params("parallel"),
        name="final_adaptive_norm",
    )(h, fmod, nw)


def _permute_w_in(w_in):
    sizes = [ATTN_W, N_KV_HEADS * ATTN_HD, N_KV_HEADS * ATTN_HD, GK_W, GK_W, GV_W, GV_W, GLA_RANK]
    idx = np.cumsum(sizes)[:-1]
    q_a, k_a, v_a, q_g, k_g, v_g, r_g, gate = jnp.split(w_in, idx, axis=-1)

    def dup(t):
        parts = []
        for kh in range(N_KV_HEADS):
            hcol = t[..., kh * ATTN_HD:(kh + 1) * ATTN_HD]
            parts += [hcol, hcol]
        return jnp.concatenate(parts, axis=-1)

    gate = jnp.pad(gate, ((0, 0), (0, 0), (0, LANES - GLA_RANK)))
    return jnp.concatenate([q_a, dup(k_a), dup(v_a), q_g, k_g, v_g, r_g, gate], axis=-1).astype(BF16)


def kernel(x, c, ada_w, ada_b, norm_ffn1, ffn1_w1, ffn1_w3, ffn1_w2, norm_mix, w_in, gla_gate_w2,
           gla_gate_b, attn_sinks, gla_norm, w_out, norm_ffn2, ffn2_w1, ffn2_w3, ffn2_w2,
           final_ada_w, final_ada_b, final_norm):
    B, S, D = x.shape
    L = ada_w.shape[0]
    T = B * S

    c_pad = jnp.pad(c, ((0, SUBLANES - B), (0, 0)))
    mod = _modulation(c_pad, ada_w, ada_b)[:, :B].reshape(L, B, N_MOD, D)
    fmod = _modulation(c_pad, final_ada_w[None], final_ada_b[None])[0, :B].reshape(B, 2, D)

    w11, w13, w12 = ffn1_w1.astype(BF16), ffn1_w3.astype(BF16), ffn1_w2.astype(BF16)
    w21, w23, w22 = ffn2_w1.astype(BF16), ffn2_w3.astype(BF16), ffn2_w2.astype(BF16)
    w_in_p = _permute_w_in(w_in)
    w_out_b = w_out.astype(BF16)
    gate_w2_p = jnp.pad(gla_gate_w2, ((0, 0), (0, LANES - GLA_RANK), (0, 0))).astype(BF16)
    gate_b = gla_gate_b.reshape(L, 1, GK_W)
    n1, nm, n2 = (t.reshape(L, 1, D) for t in (norm_ffn1, norm_mix, norm_ffn2))
    gnorm = gla_norm.reshape(L, 1, GLA_DV)
    bias_tab = jnp.asarray(_attn_bias_table())

    h = x.reshape(T, D)
    for l in range(L):
        modl = mod[l]
        h = _ffn(h, modl, n1, w11, w13, w12, l, 0, S)
        qa, kd, vd, qg, kg, vg, rg, gg = _proj(h, modl, nm, w_in_p, gate_w2_p, gate_b, l, S)
        r3 = lambda t: t.reshape(B, S, t.shape[-1])
        oa = _attention(r3(qa), r3(kd), r3(vd), attn_sinks[l], bias_tab, B, S)
        og = _gla(r3(qg), r3(kg), r3(vg), r3(gg), r3(rg), gnorm, l, B, S)
        h = _out_proj(h, modl, oa.reshape(T, ATTN_W), og.reshape(T, GV_W), w_out_b, l, S)
        h = _ffn(h, modl, n2, w21, w23, w22, l, 6, S)
    out = _final_norm(h, fmod, final_norm.reshape(1, D), S)
    return out.reshape(B, S, D)
```

```python
import functools

import jax
import jax.numpy as jnp
import numpy as np
from jax import lax
from jax.experimental import pallas as pl
from jax.experimental.pallas import tpu as pltpu

F32 = jnp.float32
BF16 = jnp.bfloat16

ATTN_HD = 64
N_Q_HEADS = 16
N_KV_HEADS = 2
GQA_GROUP = N_Q_HEADS // N_KV_HEADS
PAIRS_PER_KV = GQA_GROUP // 2
WINDOW = 128
BLOCK = 128
GLA_HEADS = 4
GLA_DK = 128
GLA_DV = 256
GLA_RANK = 16
GLA_TAU = 16.0
GLA_CHUNK = 64
N_MOD = 9
EPS = 1e-6
MASK_VALUE = -1e30

LANES = 128
SUBLANES = 8
VMEM_LIMIT_BYTES = 56 * 1024 * 1024

FFN_TM = 512
FFN_TF = 512
MIX_TM = 512
PROJ_PIECE = 256
OUT_TM = 512
FIN_TM = 512
MOD_TN = 1024

ATTN_W = N_Q_HEADS * ATTN_HD
KV_DUP_W = N_KV_HEADS * 2 * ATTN_HD
GK_W = GLA_HEADS * GLA_DK
GV_W = GLA_HEADS * GLA_DV
_SEG = {}
_off = 0
for _name, _w in (("qa", ATTN_W), ("kd", KV_DUP_W), ("vd", KV_DUP_W), ("qg", GK_W), ("kg", GK_W),
                  ("vg", GV_W), ("rg", GV_W), ("gate", LANES)):
    _SEG[_name] = (_off, _w)
    _off += _w
PROJ_W = _off


def _params(*sem):
    return pltpu.CompilerParams(dimension_semantics=sem, vmem_limit_bytes=VMEM_LIMIT_BYTES)


def _rms_mod(x, nw, scale, shift):
    y = x * lax.rsqrt(jnp.mean(x * x, axis=-1, keepdims=True) + EPS) * nw
    return y * (1.0 + scale) + shift


def _silu(x):
    return x * jax.nn.sigmoid(x)


def _mod_kernel(c_ref, w_ref, b_ref, o_ref):
    ca = _silu(c_ref[...]).astype(BF16)
    o_ref[...] = jnp.dot(ca, w_ref[...].astype(BF16), preferred_element_type=F32) + b_ref[...]


def _modulation(c_pad, w, b):
    L, D, N = w.shape
    rows = c_pad.shape[0]
    return pl.pallas_call(
        _mod_kernel,
        grid=(L, N // MOD_TN),
        in_specs=[
            pl.BlockSpec((rows, D), lambda l, j: (0, 0)),
            pl.BlockSpec((None, D, MOD_TN), lambda l, j: (l, 0, j)),
            pl.BlockSpec((None, 1, MOD_TN), lambda l, j: (l, 0, j)),
        ],
        out_specs=pl.BlockSpec((None, rows, MOD_TN), lambda l, j: (l, 0, j)),
        out_shape=jax.ShapeDtypeStruct((L, rows, N), F32),
        compiler_params=_params("parallel", "parallel"),
        name="adaln_modulation",
    )(c_pad, w, b.reshape(L, 1, N))


def _ffn_kernel(h_ref, mod_ref, nw_ref, w1_ref, w3_ref, w2_ref, o_ref, u_ref, *, row0, nf):
    f = pl.program_id(1)

    @pl.when(f == 0)
    def _():
        u = _rms_mod(h_ref[...], nw_ref[...], mod_ref[row0 + 1:row0 + 2, :], mod_ref[row0:row0 + 1, :])
        u_ref[...] = u.astype(BF16)
        o_ref[...] = jnp.zeros_like(o_ref)

    u = u_ref[...]
    a = jnp.dot(u, w1_ref[...], preferred_element_type=F32)
    b = jnp.dot(u, w3_ref[...], preferred_element_type=F32)
    hm = (_silu(a) * b).astype(BF16)
    o_ref[...] += jnp.dot(hm, w2_ref[...], preferred_element_type=F32)

    @pl.when(f == nf - 1)
    def _():
        o_ref[...] = h_ref[...] + (0.5 * mod_ref[row0 + 2:row0 + 3, :]) * o_ref[...]


def _ffn(h, modl, nw, w1, w3, w2, l, row0, seq):
    T, D = h.shape
    F = w1.shape[2]
    nf = F // FFN_TF
    tpb = seq // FFN_TM
    return pl.pallas_call(
        functools.partial(_ffn_kernel, row0=row0, nf=nf),
        grid=(T // FFN_TM, nf),
        in_specs=[
            pl.BlockSpec((FFN_TM, D), lambda i, f: (i, 0)),
            pl.BlockSpec((None, N_MOD, D), lambda i, f: (i // tpb, 0, 0)),
            pl.BlockSpec((None, 1, D), lambda i, f: (l, 0, 0)),
            pl.BlockSpec((None, D, FFN_TF), lambda i, f: (l, 0, f)),
            pl.BlockSpec((None, D, FFN_TF), lambda i, f: (l, 0, f)),
            pl.BlockSpec((None, FFN_TF, D), lambda i, f: (l, f, 0)),
        ],
        out_specs=pl.BlockSpec((FFN_TM, D), lambda i, f: (i, 0)),
        out_shape=jax.ShapeDtypeStruct((T, D), F32),
        scratch_shapes=[pltpu.VMEM((FFN_TM, D), BF16)],
        compiler_params=_params("parallel", "arbitrary"),
        name="swiglu_ffn",
    )(h, modl, nw, w1, w3, w2)


def _log_sigmoid(x):
    return jnp.minimum(x, 0.0) - jnp.log1p(jnp.exp(-jnp.abs(x)))


_BUF_FIELDS = (("qa", ATTN_W, BF16), ("kd", KV_DUP_W, BF16), ("vd", KV_DUP_W, BF16), ("qg", GK_W, F32),
               ("kg", GK_W, F32), ("vg", GV_W, BF16), ("rg", GV_W, F32), ("gg", GK_W, F32))


def _project_pieces(h_ref, mod_ref, nw_ref, w_ref, gw_ref, gb_ref, buf):
    u = _rms_mod(h_ref[...], nw_ref[...], mod_ref[4:5, :], mod_ref[3:4, :]).astype(BF16)
    for (name, width, dtype), dst in zip(_BUF_FIELDS[:-1], buf[:-1]):
        start = _SEG[name][0]
        for c0 in range(0, width, PROJ_PIECE):
            piece = jnp.dot(u, w_ref[:, start + c0:start + c0 + PROJ_PIECE], preferred_element_type=F32)
            dst[:, c0:c0 + PROJ_PIECE] = piece.astype(dtype)
            yield
    start, width = _SEG["gate"]
    gate_lr = jnp.dot(u, w_ref[:, start:start + width], preferred_element_type=F32).astype(BF16)
    z = jnp.dot(gate_lr, gw_ref[...], preferred_element_type=F32) + gb_ref[...]
    buf[-1][...] = _log_sigmoid(z) / GLA_TAU
    yield


def _attn_bias_table():
    qi = np.arange(BLOCK)[:, None]
    kj = np.arange(2 * BLOCK)[None, :]
    dist = qi + BLOCK - kj
    band = (dist >= 0) & (dist < WINDOW)
    slopes = np.exp2(-8.0 * np.arange(1, N_Q_HEADS + 1, dtype=np.float32) / N_Q_HEADS).astype(np.float32)
    tab = np.empty((N_KV_HEADS, PAIRS_PER_KV * BLOCK, 4 * BLOCK), np.float32)
    for kh in range(N_KV_HEADS):
        for p in range(PAIRS_PER_KV):
            for e in range(2):
                hd = kh * GQA_GROUP + 2 * p + e
                bias = np.where(band, -slopes[hd] * dist.astype(np.float32), np.float32(MASK_VALUE))
                tab[kh, p * BLOCK:(p + 1) * BLOCK, e * 2 * BLOCK:(e + 1) * 2 * BLOCK] = bias
    return tab


def _attention_stage(buf, tk_ref, tv_ref, sink_ref, bias_ref, first, o_ref):
    q_ref, kc_ref, vc_ref = buf[0], buf[1], buf[2]
    tm = q_ref.shape[0]
    lane = lax.broadcasted_iota(jnp.int32, (2 * BLOCK, LANES), 1)
    lo = lane < ATTN_HD
    col = lax.broadcasted_iota(jnp.int32, (1, 4 * BLOCK), 1)
    prev_cols = (col & (2 * BLOCK - 1)) < BLOCK
    first_mask = jnp.where(jnp.logical_and(first, prev_cols), MASK_VALUE, 0.0)
    for j in range(tm // BLOCK):
        rows = slice(j * BLOCK, (j + 1) * BLOCK)
        for kh in range(N_KV_HEADS):
            cols = slice(kh * LANES, (kh + 1) * LANES)
            if j == 0:
                kprev, vprev = tk_ref[:, cols], tv_ref[:, cols]
            else:
                prows = slice((j - 1) * BLOCK, j * BLOCK)
                kprev, vprev = kc_ref[prows, cols], vc_ref[prows, cols]
            kk = jnp.concatenate([kprev, kc_ref[rows, cols]], axis=0)
            vv = jnp.concatenate([vprev, vc_ref[rows, cols]], axis=0)
            zero = jnp.zeros_like(kk)
            kbd = jnp.concatenate([jnp.where(lo, kk, zero), jnp.where(lo, zero, kk)], axis=0)
            vbd = jnp.concatenate([jnp.where(lo, vv, zero), jnp.where(lo, zero, vv)], axis=0)
            qs = jnp.concatenate(
                [q_ref[rows, (kh * PAIRS_PER_KV + p) * LANES:(kh * PAIRS_PER_KV + p + 1) * LANES]
                 for p in range(PAIRS_PER_KV)], axis=0)
            s = lax.dot_general(qs, kbd, (((1,), (1,)), ((), ())), preferred_element_type=F32)
            s = s * (ATTN_HD ** -0.5) + bias_ref[kh]
            if j == 0:
                s = s + first_mask
            prow = []
            for p in range(PAIRS_PER_KV):
                pe = []
                for e in range(2):
                    sink = sink_ref[kh * GQA_GROUP + 2 * p + e]
                    t = s[p * BLOCK:(p + 1) * BLOCK, e * 2 * BLOCK:(e + 1) * 2 * BLOCK]
                    m = jnp.maximum(jnp.max(t, axis=-1, keepdims=True), sink)
                    pexp = jnp.exp(t - m)
                    den = jnp.sum(pexp, axis=-1, keepdims=True) + jnp.exp(sink - m)
                    pe.append((pexp / den).astype(BF16))
                prow.append(jnp.concatenate(pe, axis=1))
            pmat = jnp.concatenate(prow, axis=0)
            o = jnp.dot(pmat, vbd, preferred_element_type=F32)
            for p in range(PAIRS_PER_KV):
                c0 = (kh * PAIRS_PER_KV + p) * LANES
                o_ref[rows, c0:c0 + LANES] = o[p * BLOCK:(p + 1) * BLOCK].astype(BF16)
            yield
    tk_ref[...] = kc_ref[tm - BLOCK:tm, :]
    tv_ref[...] = vc_ref[tm - BLOCK:tm, :]


def _gla_stage(buf, nw_ref, st_ref, o_ref):
    q_ref, k_ref, v_ref, r_ref, g_ref = buf[3], buf[4], buf[5], buf[6], buf[7]
    tm = q_ref.shape[0]
    ci = lax.broadcasted_iota(jnp.int32, (GLA_CHUNK, GLA_CHUNK), 0)
    si = lax.broadcasted_iota(jnp.int32, (GLA_CHUNK, GLA_CHUNK), 1)
    causal = si <= ci
    tri = causal.astype(BF16)
    nw = nw_ref[...]
    for c in range(tm // GLA_CHUNK):
        rows = slice(c * GLA_CHUNK, (c + 1) * GLA_CHUNK)
        g = g_ref[rows, :]
        g_hi = g.astype(BF16)
        g_r1 = g - g_hi.astype(F32)
        g_mid = g_r1.astype(BF16)
        g_lo = (g_r1 - g_mid.astype(F32)).astype(BF16)
        b = (jnp.dot(tri, g_hi, preferred_element_type=F32)
             + jnp.dot(tri, g_mid, preferred_element_type=F32)
             + jnp.dot(tri, g_lo, preferred_element_type=F32))
        b_last = b[GLA_CHUNK - 1:GLA_CHUNK, :]
        q = q_ref[rows, :] * (GLA_DK ** -0.5)
        k = k_ref[rows, :]
        q_t = (q * jnp.exp(b)).astype(BF16)
        k_t = (k * jnp.exp(-b)).astype(BF16)
        k_end = (k * jnp.exp(b_last - b)).astype(BF16)
        decay = jnp.exp(b_last)
        for hd in range(GLA_HEADS):
            kc = slice(hd * GLA_DK, (hd + 1) * GLA_DK)
            vc = slice(hd * GLA_DV, (hd + 1) * GLA_DV)
            v = v_ref[rows, vc]
            a = lax.dot_general(q_t[:, kc], k_t[:, kc], (((1,), (1,)), ((), ())),
                                preferred_element_type=F32)
            a = jnp.where(causal, a, 0.0).astype(BF16)
            st = st_ref[hd]
            o = jnp.dot(a, v, preferred_element_type=F32)
            o = o + lax.dot_general(q_t[:, kc], st.astype(BF16), (((1,), (1,)), ((), ())),
                                    preferred_element_type=F32)
            upd = lax.dot_general(v, k_end[:, kc], (((0,), (0,)), ((), ())),
                                  preferred_element_type=F32)
            st_ref[hd] = st * decay[:, kc] + upd
            o = o * lax.rsqrt(jnp.mean(o * o, axis=-1, keepdims=True) + EPS) * nw
            o_ref[rows, vc] = (o * _silu(r_ref[rows, vc])).astype(BF16)
        yield


def _mixer_kernel(sink_ref, h_ref, mod_ref, nw_ref, w_ref, gw_ref, gb_ref, bias_ref, gnw_ref,
                  oa_ref, og_ref, *scratch):
    nb = len(_BUF_FIELDS)
    bufs = (scratch[:nb], scratch[nb:2 * nb])
    tk_ref, tv_ref, st_ref = scratch[2 * nb:]
    s = pl.program_id(1)

    @pl.when(s == 0)
    def _():
        for r in bufs[1] + (tk_ref, tv_ref, st_ref):
            r[...] = jnp.zeros_like(r)

    def step(project_into, mix_from):
        pieces = _project_pieces(h_ref, mod_ref, nw_ref, w_ref, gw_ref, gb_ref, project_into)
        next(pieces)
        for _ in _attention_stage(mix_from, tk_ref, tv_ref, sink_ref, bias_ref, s == 1, oa_ref):
            next(pieces, None)
        for _ in _gla_stage(mix_from, gnw_ref, st_ref, og_ref):
            next(pieces, None)
        for _ in pieces:
            pass

    @pl.when((s & 1) == 0)
    def _():
        step(bufs[0], bufs[1])

    @pl.when((s & 1) == 1)
    def _():
        step(bufs[1], bufs[0])


def _mixer_core(h3, modl, nw, w_in_p, gate_w2_p, gate_b, sinks, bias_tab, gnorm, l):
    B, S, D = h3.shape
    nt = S // MIX_TM
    single = pl.Buffered(1)
    proj_tile = lambda b, s: (b, jnp.minimum(s, nt - 1), 0)
    mix_tile = lambda b, s: (b, jnp.maximum(s - 1, 0), 0)
    scratch = [pltpu.VMEM((MIX_TM, w), dt) for _ in range(2) for _, w, dt in _BUF_FIELDS]
    scratch += [pltpu.VMEM((BLOCK, KV_DUP_W), BF16), pltpu.VMEM((BLOCK, KV_DUP_W), BF16),
                pltpu.VMEM((GLA_HEADS, GLA_DV, GLA_DK), F32)]
    return pl.pallas_call(
        _mixer_kernel,
        grid=(B, nt + 1),
        in_specs=[
            pl.BlockSpec(memory_space=pltpu.SMEM),
            pl.BlockSpec((None, MIX_TM, D), proj_tile),
            pl.BlockSpec((None, N_MOD, D), lambda b, s: (b, 0, 0)),
            pl.BlockSpec((None, 1, D), lambda b, s: (l, 0, 0)),
            pl.BlockSpec((None, D, PROJ_W), lambda b, s: (l, 0, 0), pipeline_mode=single),
            pl.BlockSpec((None, LANES, GK_W), lambda b, s: (l, 0, 0)),
            pl.BlockSpec((None, 1, GK_W), lambda b, s: (l, 0, 0)),
            pl.BlockSpec(bias_tab.shape, lambda b, s: (0, 0, 0), pipeline_mode=single),
            pl.BlockSpec((None, 1, GLA_DV), lambda b, s: (l, 0, 0)),
        ],
        out_specs=[pl.BlockSpec((None, MIX_TM, ATTN_W), mix_tile),
                   pl.BlockSpec((None, MIX_TM, GV_W), mix_tile)],
        out_shape=[jax.ShapeDtypeStruct((B, S, ATTN_W), BF16), jax.ShapeDtypeStruct((B, S, GV_W), BF16)],
        scratch_shapes=scratch,
        compiler_params=_params("parallel", "arbitrary"),
        name="mixer_core",
    )(sinks, h3, modl, nw, w_in_p, gate_w2_p, gate_b, bias_tab, gnorm)


def _out_kernel(h_ref, mod_ref, oa_ref, og_ref, w_ref, o_ref):
    ka = oa_ref.shape[1]
    y = jnp.dot(oa_ref[...], w_ref[:ka, :], preferred_element_type=F32)
    y = y + jnp.dot(og_ref[...], w_ref[ka:, :], preferred_element_type=F32)
    o_ref[...] = h_ref[...] + mod_ref[5:6, :] * y


def _out_proj(h, modl, oa, og, w_out, l, seq):
    T, D = h.shape
    tpb = seq // OUT_TM
    return pl.pallas_call(
        _out_kernel,
        grid=(T // OUT_TM,),
        in_specs=[
            pl.BlockSpec((OUT_TM, D), lambda i: (i, 0)),
            pl.BlockSpec((None, N_MOD, D), lambda i: (i // tpb, 0, 0)),
            pl.BlockSpec((OUT_TM, oa.shape[1]), lambda i: (i, 0)),
            pl.BlockSpec((OUT_TM, og.shape[1]), lambda i: (i, 0)),
            pl.BlockSpec((None,) + w_out.shape[1:], lambda i: (l, 0, 0)),
        ],
        out_specs=pl.BlockSpec((OUT_TM, D), lambda i: (i, 0)),
        out_shape=jax.ShapeDtypeStruct((T, D), F32),
        compiler_params=_params("parallel"),
        name="mixer_out_proj",
    )(h, modl, oa, og, w_out)


def _final_kernel(h_ref, mod_ref, nw_ref, o_ref):
    o_ref[...] = _rms_mod(h_ref[...], nw_ref[...], mod_ref[1:2, :], mod_ref[0:1, :])


def _final_norm(h, fmod, nw, seq):
    T, D = h.shape
    tpb = seq // FIN_TM
    return pl.pallas_call(
        _final_kernel,
        grid=(T // FIN_TM,),
        in_specs=[
            pl.BlockSpec((FIN_TM, D), lambda i: (i, 0)),
            pl.BlockSpec((None, 2, D), lambda i: (i // tpb, 0, 0)),
            pl.BlockSpec((1, D), lambda i: (0, 0)),
        ],
        out_specs=pl.BlockSpec((FIN_TM, D), lambda i: (i, 0)),
        out_shape=jax.ShapeDtypeStruct((T, D), F32),
        compiler_params=_params("parallel"),
        name="final_adaptive_norm",
    )(h, fmod, nw)


def _permute_w_in(w_in):
    sizes = [ATTN_W, N_KV_HEADS * ATTN_HD, N_KV_HEADS * ATTN_HD, GK_W, GK_W, GV_W, GV_W, GLA_RANK]
    idx = np.cumsum(sizes)[:-1]
    q_a, k_a, v_a, q_g, k_g, v_g, r_g, gate = jnp.split(w_in, idx, axis=-1)

    def dup(t):
        parts = []
        for kh in range(N_KV_HEADS):
            hcol = t[..., kh * ATTN_HD:(kh + 1) * ATTN_HD]
            parts += [hcol, hcol]
        return jnp.concatenate(parts, axis=-1)

    gate = jnp.pad(gate, ((0, 0), (0, 0), (0, LANES - GLA_RANK)))
    return jnp.concatenate([q_a, dup(k_a), dup(v_a), q_g, k_g, v_g, r_g, gate], axis=-1).astype(BF16)


def kernel(x, c, ada_w, ada_b, norm_ffn1, ffn1_w1, ffn1_w3, ffn1_w2, norm_mix, w_in, gla_gate_w2,
           gla_gate_b, attn_sinks, gla_norm, w_out, norm_ffn2, ffn2_w1, ffn2_w3, ffn2_w2,
           final_ada_w, final_ada_b, final_norm):
    B, S, D = x.shape
    L = ada_w.shape[0]
    T = B * S

    c_pad = jnp.pad(c, ((0, SUBLANES - B), (0, 0)))
    mod = _modulation(c_pad, ada_w, ada_b)[:, :B].reshape(L, B, N_MOD, D)
    fmod = _modulation(c_pad, final_ada_w[None], final_ada_b[None])[0, :B].reshape(B, 2, D)

    w11, w13, w12 = ffn1_w1.astype(BF16), ffn1_w3.astype(BF16), ffn1_w2.astype(BF16)
    w21, w23, w22 = ffn2_w1.astype(BF16), ffn2_w3.astype(BF16), ffn2_w2.astype(BF16)
    w_in_p = _permute_w_in(w_in)
    w_out_b = w_out.astype(BF16)
    gate_w2_p = jnp.pad(gla_gate_w2, ((0, 0), (0, LANES - GLA_RANK), (0, 0))).astype(BF16)
    gate_b = gla_gate_b.reshape(L, 1, GK_W)
    n1, nm, n2 = (t.reshape(L, 1, D) for t in (norm_ffn1, norm_mix, norm_ffn2))
    gnorm = gla_norm.reshape(L, 1, GLA_DV)
    bias_tab = jnp.asarray(_attn_bias_table())

    h = x.reshape(T, D)
    for l in range(L):
        modl = mod[l]
        h = _ffn(h, modl, n1, w11, w13, w12, l, 0, S)
        oa, og = _mixer_core(h.reshape(B, S, D), modl, nm, w_in_p, gate_w2_p, gate_b, attn_sinks[l],
                             bias_tab, gnorm, l)
        h = _out_proj(h, modl, oa.reshape(T, ATTN_W), og.reshape(T, GV_W), w_out_b, l, S)
        h = _ffn(h, modl, n2, w21, w23, w22, l, 6, S)
    out = _final_norm(h, fmod, final_norm.reshape(1, D), S)
    return out.reshape(B, S, D)
```

```python
import functools

import jax
import jax.numpy as jnp
import numpy as np
from jax import lax
from jax.experimental import pallas as pl
from jax.experimental.pallas import tpu as pltpu

F32 = jnp.float32
BF16 = jnp.bfloat16

ATTN_HD = 64
N_Q_HEADS = 16
N_KV_HEADS = 2
GQA_GROUP = N_Q_HEADS // N_KV_HEADS
PAIRS_PER_KV = GQA_GROUP // 2
WINDOW = 128
BLOCK = 128
GLA_HEADS = 4
GLA_DK = 128
GLA_DV = 256
GLA_RANK = 16
GLA_TAU = 16.0
GLA_CHUNK = 64
N_MOD = 9
EPS = 1e-6
MASK_VALUE = -1e30

LANES = 128
SUBLANES = 8
VMEM_LIMIT_BYTES = 56 * 1024 * 1024

FFN_TM = 512
FFN_TF = 512
FFN_NORM_SLICES = 8
FFN_NORM_ROWS = FFN_TM // FFN_NORM_SLICES
MIX_TM = 512
PROJ_PIECE = 256
OUT_TM = 512
MOD_TN = 1024

ATTN_W = N_Q_HEADS * ATTN_HD
KV_DUP_W = N_KV_HEADS * 2 * ATTN_HD
GK_W = GLA_HEADS * GLA_DK
GV_W = GLA_HEADS * GLA_DV
_SEG = {}
_off = 0
for _name, _w in (("qa", ATTN_W), ("kv", 2 * N_KV_HEADS * ATTN_HD), ("qg", GK_W), ("kg", GK_W),
                  ("vg", GV_W), ("rg", GV_W), ("gate", GLA_RANK)):
    _SEG[_name] = (_off, _w)
    _off += _w
PROJ_W = _off


def _params(*sem):
    return pltpu.CompilerParams(dimension_semantics=sem, vmem_limit_bytes=VMEM_LIMIT_BYTES)


def _rms_mod(x, nw, scale, shift):
    y = x * lax.rsqrt(jnp.mean(x * x, axis=-1, keepdims=True) + EPS) * nw
    return y * (1.0 + scale) + shift


def _silu(x):
    return x * jax.nn.sigmoid(x)


def _mod_kernel(c_ref, w_ref, b_ref, o_ref):
    ca = _silu(c_ref[...]).astype(BF16)
    o_ref[...] = jnp.dot(ca, w_ref[...].astype(BF16), preferred_element_type=F32) + b_ref[...]


def _modulation(c_pad, w, b):
    L, D, N = w.shape
    rows = c_pad.shape[0]
    return pl.pallas_call(
        _mod_kernel,
        grid=(L, N // MOD_TN),
        in_specs=[
            pl.BlockSpec((rows, D), lambda l, j: (0, 0)),
            pl.BlockSpec((None, D, MOD_TN), lambda l, j: (l, 0, j)),
            pl.BlockSpec((None, 1, MOD_TN), lambda l, j: (l, 0, j)),
        ],
        out_specs=pl.BlockSpec((None, rows, MOD_TN), lambda l, j: (l, 0, j)),
        out_shape=jax.ShapeDtypeStruct((L, rows, N), F32),
        compiler_params=_params("parallel", "parallel"),
        name="adaln_modulation",
    )(c_pad, w, b.reshape(L, 1, N))


def _ffn_kernel(hc_ref, hn_ref, modc_ref, modn_ref, nw_ref, w1_ref, w3_ref, w2_ref, fmod_ref, fnw_ref,
                o_ref, u0_ref, u1_ref, *, row0, nf, final):
    i = pl.program_id(0)
    f = pl.program_id(1)

    def normed(h_ref, mod_ref, rows):
        u = _rms_mod(h_ref[rows, :], nw_ref[...], mod_ref[row0 + 1:row0 + 2, :], mod_ref[row0:row0 + 1, :])
        return u.astype(BF16)

    @pl.when(jnp.logical_and(i == 0, f == 0))
    def _():
        u0_ref[...] = normed(hc_ref, modc_ref, slice(None))

    @pl.when(f == 0)
    def _():
        o_ref[...] = jnp.zeros_like(o_ref)

    def step(u_cur, u_nxt):
        r0 = pl.multiple_of(jnp.minimum(f, FFN_NORM_SLICES - 1) * FFN_NORM_ROWS, FFN_NORM_ROWS)
        rows = pl.ds(r0, FFN_NORM_ROWS)
        u_nxt[rows, :] = normed(hn_ref, modn_ref, rows)
        u = u_cur[...]
        a = jnp.dot(u, w1_ref[...], preferred_element_type=F32)
        b = jnp.dot(u, w3_ref[...], preferred_element_type=F32)
        hm = (_silu(a) * b).astype(BF16)
        o_ref[...] += jnp.dot(hm, w2_ref[...], preferred_element_type=F32)

    @pl.when((i & 1) == 0)
    def _():
        step(u0_ref, u1_ref)

    @pl.when((i & 1) == 1)
    def _():
        step(u1_ref, u0_ref)

    @pl.when(f == nf - 1)
    def _():
        y = hc_ref[...] + (0.5 * modc_ref[row0 + 2:row0 + 3, :]) * o_ref[...]
        if final:
            y = _rms_mod(y, fnw_ref[...], fmod_ref[1:2, :], fmod_ref[0:1, :])
        o_ref[...] = y


def _ffn(h, modl, nw, w1, w3, w2, fmod, fnw, l, row0, seq, final=False):
    T, D = h.shape
    F = w1.shape[2]
    nf = F // FFN_TF
    nt = T // FFN_TM
    tpb = seq // FFN_TM
    assert nf >= FFN_NORM_SLICES
    nxt = lambda i: jnp.minimum(i + 1, nt - 1)
    return pl.pallas_call(
        functools.partial(_ffn_kernel, row0=row0, nf=nf, final=final),
        grid=(nt, nf),
        in_specs=[
            pl.BlockSpec((FFN_TM, D), lambda i, f: (i, 0)),
            pl.BlockSpec((FFN_TM, D), lambda i, f: (nxt(i), 0)),
            pl.BlockSpec((None, N_MOD, D), lambda i, f: (i // tpb, 0, 0)),
            pl.BlockSpec((None, N_MOD, D), lambda i, f: (nxt(i) // tpb, 0, 0)),
            pl.BlockSpec((None, 1, D), lambda i, f: (l, 0, 0)),
            pl.BlockSpec((None, D, FFN_TF), lambda i, f: (l, 0, f)),
            pl.BlockSpec((None, D, FFN_TF), lambda i, f: (l, 0, f)),
            pl.BlockSpec((None, FFN_TF, D), lambda i, f: (l, f, 0)),
            pl.BlockSpec((None, 2, D), lambda i, f: (i // tpb, 0, 0)),
            pl.BlockSpec((1, D), lambda i, f: (0, 0)),
        ],
        out_specs=pl.BlockSpec((FFN_TM, D), lambda i, f: (i, 0)),
        out_shape=jax.ShapeDtypeStruct((T, D), F32),
        scratch_shapes=[pltpu.VMEM((FFN_TM, D), BF16), pltpu.VMEM((FFN_TM, D), BF16)],
        compiler_params=_params("arbitrary", "arbitrary"),
        name="swiglu_ffn",
    )(h, h, modl, modl, nw, w1, w3, w2, fmod, fnw)


def _log_sigmoid(x):
    return jnp.minimum(x, 0.0) - jnp.log1p(jnp.exp(-jnp.abs(x)))


_BUF_FIELDS = (("qa", ATTN_W, BF16), ("kd", KV_DUP_W, BF16), ("vd", KV_DUP_W, BF16), ("qg", GK_W, F32),
               ("kg", GK_W, F32), ("vg", GV_W, BF16), ("rg", GV_W, F32), ("gg", GK_W, F32))


def _project_pieces(h_ref, mod_ref, nw_ref, w_ref, wg_ref, gw_ref, gb_ref, buf):
    qa_ref, kd_ref, vd_ref, qg_ref, kg_ref, vg_ref, rg_ref, gg_ref = buf
    u = _rms_mod(h_ref[...], nw_ref[...], mod_ref[4:5, :], mod_ref[3:4, :]).astype(BF16)

    def piece(c0):
        return jnp.dot(u, w_ref[:, c0:c0 + PROJ_PIECE], preferred_element_type=F32)

    def plain(name, dst):
        start, width = _SEG[name]
        for c0 in range(0, width, PROJ_PIECE):
            dst[:, c0:c0 + PROJ_PIECE] = piece(start + c0).astype(dst.dtype)
            yield

    yield from plain("qa", qa_ref)
    kv = piece(_SEG["kv"][0])
    lo = lax.broadcasted_iota(jnp.int32, (kv.shape[0], LANES), 1) < ATTN_HD
    for src, dst in ((kv[:, :LANES], kd_ref), (kv[:, LANES:], vd_ref)):
        swapped = pltpu.roll(src, ATTN_HD, axis=1)
        dst[:, :LANES] = jnp.where(lo, src, swapped).astype(BF16)
        dst[:, LANES:] = jnp.where(lo, swapped, src).astype(BF16)
    yield
    yield from plain("qg", qg_ref)
    yield from plain("kg", kg_ref)
    yield from plain("vg", vg_ref)
    yield from plain("rg", rg_ref)
    gate_lr = jnp.dot(u, wg_ref[...], preferred_element_type=F32).astype(BF16)
    z = jnp.dot(gate_lr, gw_ref[...], preferred_element_type=F32) + gb_ref[...]
    gg_ref[...] = _log_sigmoid(z) / GLA_TAU
    yield


def _attn_bias_table():
    qi = np.arange(BLOCK)[:, None]
    kj = np.arange(2 * BLOCK)[None, :]
    dist = qi + BLOCK - kj
    band = (dist >= 0) & (dist < WINDOW)
    slopes = np.exp2(-8.0 * np.arange(1, N_Q_HEADS + 1, dtype=np.float32) / N_Q_HEADS).astype(np.float32)
    tab = np.empty((N_KV_HEADS, PAIRS_PER_KV * BLOCK, 4 * BLOCK), np.float32)
    for kh in range(N_KV_HEADS):
        for p in range(PAIRS_PER_KV):
            for e in range(2):
                hd = kh * GQA_GROUP + 2 * p + e
                bias = np.where(band, -slopes[hd] * dist.astype(np.float32), np.float32(MASK_VALUE))
                tab[kh, p * BLOCK:(p + 1) * BLOCK, e * 2 * BLOCK:(e + 1) * 2 * BLOCK] = bias
    return tab


def _attention_stage(buf, tk_ref, tv_ref, sink_ref, bias_ref, first, o_ref):
    q_ref, kc_ref, vc_ref = buf[0], buf[1], buf[2]
    tm = q_ref.shape[0]
    lane = lax.broadcasted_iota(jnp.int32, (2 * BLOCK, LANES), 1)
    lo = lane < ATTN_HD
    col = lax.broadcasted_iota(jnp.int32, (1, 4 * BLOCK), 1)
    prev_cols = (col & (2 * BLOCK - 1)) < BLOCK
    first_mask = jnp.where(jnp.logical_and(first, prev_cols), MASK_VALUE, 0.0)
    for j in range(tm // BLOCK):
        rows = slice(j * BLOCK, (j + 1) * BLOCK)
        for kh in range(N_KV_HEADS):
            cols = slice(kh * LANES, (kh + 1) * LANES)
            if j == 0:
                kprev, vprev = tk_ref[:, cols], tv_ref[:, cols]
            else:
                prows = slice((j - 1) * BLOCK, j * BLOCK)
                kprev, vprev = kc_ref[prows, cols], vc_ref[prows, cols]
            kk = jnp.concatenate([kprev, kc_ref[rows, cols]], axis=0)
            vv = jnp.concatenate([vprev, vc_ref[rows, cols]], axis=0)
            zero = jnp.zeros_like(kk)
            kbd = jnp.concatenate([jnp.where(lo, kk, zero), jnp.where(lo, zero, kk)], axis=0)
            vbd = jnp.concatenate([jnp.where(lo, vv, zero), jnp.where(lo, zero, vv)], axis=0)
            qs = jnp.concatenate(
                [q_ref[rows, (kh * PAIRS_PER_KV + p) * LANES:(kh * PAIRS_PER_KV + p + 1) * LANES]
                 for p in range(PAIRS_PER_KV)], axis=0)
            s = lax.dot_general(qs, kbd, (((1,), (1,)), ((), ())), preferred_element_type=F32)
            s = s * (ATTN_HD ** -0.5) + bias_ref[kh]
            if j == 0:
                s = s + first_mask
            prow = []
            for p in range(PAIRS_PER_KV):
                pe = []
                for e in range(2):
                    sink = sink_ref[kh * GQA_GROUP + 2 * p + e]
                    t = s[p * BLOCK:(p + 1) * BLOCK, e * 2 * BLOCK:(e + 1) * 2 * BLOCK]
                    m = jnp.maximum(jnp.max(t, axis=-1, keepdims=True), sink)
                    pexp = jnp.exp(t - m)
                    den = jnp.sum(pexp, axis=-1, keepdims=True) + jnp.exp(sink - m)
                    pe.append((pexp / den).astype(BF16))
                prow.append(jnp.concatenate(pe, axis=1))
            pmat = jnp.concatenate(prow, axis=0)
            o = jnp.dot(pmat, vbd, preferred_element_type=F32)
            for p in range(PAIRS_PER_KV):
                c0 = (kh * PAIRS_PER_KV + p) * LANES
                o_ref[rows, c0:c0 + LANES] = o[p * BLOCK:(p + 1) * BLOCK].astype(BF16)
            yield
    tk_ref[...] = kc_ref[tm - BLOCK:tm, :]
    tv_ref[...] = vc_ref[tm - BLOCK:tm, :]


def _gla_stage(buf, nw_ref, st_ref, o_ref):
    q_ref, k_ref, v_ref, r_ref, g_ref = buf[3], buf[4], buf[5], buf[6], buf[7]
    tm = q_ref.shape[0]
    ci = lax.broadcasted_iota(jnp.int32, (GLA_CHUNK, GLA_CHUNK), 0)
    si = lax.broadcasted_iota(jnp.int32, (GLA_CHUNK, GLA_CHUNK), 1)
    causal = si <= ci
    tri = causal.astype(BF16)
    nw = nw_ref[...]
    for c in range(tm // GLA_CHUNK):
        rows = slice(c * GLA_CHUNK, (c + 1) * GLA_CHUNK)
        g = g_ref[rows, :]
        g_hi = g.astype(BF16)
        g_r1 = g - g_hi.astype(F32)
        g_mid = g_r1.astype(BF16)
        g_lo = (g_r1 - g_mid.astype(F32)).astype(BF16)
        b = (jnp.dot(tri, g_hi, preferred_element_type=F32)
             + jnp.dot(tri, g_mid, preferred_element_type=F32)
             + jnp.dot(tri, g_lo, preferred_element_type=F32))
        b_last = b[GLA_CHUNK - 1:GLA_CHUNK, :]
        q = q_ref[rows, :] * (GLA_DK ** -0.5)
        k = k_ref[rows, :]
        q_t = (q * jnp.exp(b)).astype(BF16)
        k_t = (k * jnp.exp(-b)).astype(BF16)
        k_end = (k * jnp.exp(b_last - b)).astype(BF16)
        decay = jnp.exp(b_last)
        for hd in range(GLA_HEADS):
            kc = slice(hd * GLA_DK, (hd + 1) * GLA_DK)
            vc = slice(hd * GLA_DV, (hd + 1) * GLA_DV)
            v = v_ref[rows, vc]
            a = lax.dot_general(q_t[:, kc], k_t[:, kc], (((1,), (1,)), ((), ())),
                                preferred_element_type=F32)
            a = jnp.where(causal, a, 0.0).astype(BF16)
            st = st_ref[hd]
            o = jnp.dot(a, v, preferred_element_type=F32)
            o = o + lax.dot_general(q_t[:, kc], st.astype(BF16), (((1,), (1,)), ((), ())),
                                    preferred_element_type=F32)
            upd = lax.dot_general(v, k_end[:, kc], (((0,), (0,)), ((), ())),
                                  preferred_element_type=F32)
            st_ref[hd] = st * decay[:, kc] + upd
            o = o * lax.rsqrt(jnp.mean(o * o, axis=-1, keepdims=True) + EPS) * nw
            o_ref[rows, vc] = (o * _silu(r_ref[rows, vc])).astype(BF16)
        yield


def _mixer_kernel(sink_ref, h_ref, mod_ref, nw_ref, w_ref, wg_ref, gw_ref, gb_ref, bias_ref, gnw_ref,
                  oa_ref, og_ref, *scratch):
    nb = len(_BUF_FIELDS)
    bufs = (scratch[:nb], scratch[nb:2 * nb])
    tk_ref, tv_ref, st_ref = scratch[2 * nb:]
    s = pl.program_id(1)

    @pl.when(s == 0)
    def _():
        for r in bufs[1] + (tk_ref, tv_ref, st_ref):
            r[...] = jnp.zeros_like(r)

    def step(project_into, mix_from):
        pieces = _project_pieces(h_ref, mod_ref, nw_ref, w_ref, wg_ref, gw_ref, gb_ref, project_into)
        next(pieces)
        for _ in _attention_stage(mix_from, tk_ref, tv_ref, sink_ref, bias_ref, s == 1, oa_ref):
            next(pieces, None)
        for _ in _gla_stage(mix_from, gnw_ref, st_ref, og_ref):
            next(pieces, None)
        for _ in pieces:
            pass

    @pl.when((s & 1) == 0)
    def _():
        step(bufs[0], bufs[1])

    @pl.when((s & 1) == 1)
    def _():
        step(bufs[1], bufs[0])


def _mixer_core(h3, modl, nw, w_in_b, w_gate_p, gate_w2_p, gate_b, sinks, bias_tab, gnorm, l):
    B, S, D = h3.shape
    nt = S // MIX_TM
    single = pl.Buffered(1)
    proj_tile = lambda b, s: (b, jnp.minimum(s, nt - 1), 0)
    mix_tile = lambda b, s: (b, jnp.maximum(s - 1, 0), 0)
    scratch = [pltpu.VMEM((MIX_TM, w), dt) for _ in range(2) for _, w, dt in _BUF_FIELDS]
    scratch += [pltpu.VMEM((BLOCK, KV_DUP_W), BF16), pltpu.VMEM((BLOCK, KV_DUP_W), BF16),
                pltpu.VMEM((GLA_HEADS, GLA_DV, GLA_DK), F32)]
    return pl.pallas_call(
        _mixer_kernel,
        grid=(B, nt + 1),
        in_specs=[
            pl.BlockSpec(memory_space=pltpu.SMEM),
            pl.BlockSpec((None, MIX_TM, D), proj_tile),
            pl.BlockSpec((None, N_MOD, D), lambda b, s: (b, 0, 0)),
            pl.BlockSpec((None, 1, D), lambda b, s: (l, 0, 0)),
            pl.BlockSpec((None, D, PROJ_W), lambda b, s: (l, 0, 0), pipeline_mode=single),
            pl.BlockSpec((None, D, LANES), lambda b, s: (l, 0, 0), pipeline_mode=single),
            pl.BlockSpec((None, LANES, GK_W), lambda b, s: (l, 0, 0)),
            pl.BlockSpec((None, 1, GK_W), lambda b, s: (l, 0, 0)),
            pl.BlockSpec(bias_tab.shape, lambda b, s: (0, 0, 0), pipeline_mode=single),
            pl.BlockSpec((None, 1, GLA_DV), lambda b, s: (l, 0, 0)),
        ],
        out_specs=[pl.BlockSpec((None, MIX_TM, ATTN_W), mix_tile),
                   pl.BlockSpec((None, MIX_TM, GV_W), mix_tile)],
        out_shape=[jax.ShapeDtypeStruct((B, S, ATTN_W), BF16), jax.ShapeDtypeStruct((B, S, GV_W), BF16)],
        scratch_shapes=scratch,
        compiler_params=_params("parallel", "arbitrary"),
        name="mixer_core",
    )(sinks, h3, modl, nw, w_in_b, w_gate_p, gate_w2_p, gate_b, bias_tab, gnorm)


def _out_kernel(h_ref, mod_ref, oa_ref, og_ref, w_ref, o_ref):
    ka = oa_ref.shape[1]
    y = jnp.dot(oa_ref[...], w_ref[:ka, :], preferred_element_type=F32)
    y = y + jnp.dot(og_ref[...], w_ref[ka:, :], preferred_element_type=F32)
    o_ref[...] = h_ref[...] + mod_ref[5:6, :] * y


def _out_proj(h, modl, oa, og, w_out, l, seq):
    T, D = h.shape
    tpb = seq // OUT_TM
    return pl.pallas_call(
        _out_kernel,
        grid=(T // OUT_TM,),
        in_specs=[
            pl.BlockSpec((OUT_TM, D), lambda i: (i, 0)),
            pl.BlockSpec((None, N_MOD, D), lambda i: (i // tpb, 0, 0)),
            pl.BlockSpec((OUT_TM, oa.shape[1]), lambda i: (i, 0)),
            pl.BlockSpec((OUT_TM, og.shape[1]), lambda i: (i, 0)),
            pl.BlockSpec((None,) + w_out.shape[1:], lambda i: (l, 0, 0)),
        ],
        out_specs=pl.BlockSpec((OUT_TM, D), lambda i: (i, 0)),
        out_shape=jax.ShapeDtypeStruct((T, D), F32),
        compiler_params=_params("parallel"),
        name="mixer_out_proj",
    )(h, modl, oa, og, w_out)


def kernel(x, c, ada_w, ada_b, norm_ffn1, ffn1_w1, ffn1_w3, ffn1_w2, norm_mix, w_in, gla_gate_w2,
           gla_gate_b, attn_sinks, gla_norm, w_out, norm_ffn2, ffn2_w1, ffn2_w3, ffn2_w2,
           final_ada_w, final_ada_b, final_norm):
    B, S, D = x.shape
    L = ada_w.shape[0]
    T = B * S

    c_pad = jnp.pad(c, ((0, SUBLANES - B), (0, 0)))
    mod = _modulation(c_pad, ada_w, ada_b)[:, :B].reshape(L, B, N_MOD, D)
    fmod = _modulation(c_pad, final_ada_w[None], final_ada_b[None])[0, :B].reshape(B, 2, D)

    w11, w13, w12 = ffn1_w1.astype(BF16), ffn1_w3.astype(BF16), ffn1_w2.astype(BF16)
    w21, w23, w22 = ffn2_w1.astype(BF16), ffn2_w3.astype(BF16), ffn2_w2.astype(BF16)
    assert w_in.shape[-1] == PROJ_W
    w_in_b = w_in.astype(BF16)
    gate0 = _SEG["gate"][0]
    w_gate_p = jnp.pad(w_in[:, :, gate0:], ((0, 0), (0, 0), (0, LANES - GLA_RANK))).astype(BF16)
    w_out_b = w_out.astype(BF16)
    gate_w2_p = jnp.pad(gla_gate_w2, ((0, 0), (0, LANES - GLA_RANK), (0, 0))).astype(BF16)
    gate_b = gla_gate_b.reshape(L, 1, GK_W)
    n1, nm, n2 = (t.reshape(L, 1, D) for t in (norm_ffn1, norm_mix, norm_ffn2))
    gnorm = gla_norm.reshape(L, 1, GLA_DV)
    bias_tab = jnp.asarray(_attn_bias_table())

    fnw = final_norm.reshape(1, D)
    h = x.reshape(T, D)
    for l in range(L):
        modl = mod[l]
        h = _ffn(h, modl, n1, w11, w13, w12, fmod, fnw, l, 0, S)
        oa, og = _mixer_core(h.reshape(B, S, D), modl, nm, w_in_b, w_gate_p, gate_w2_p, gate_b,
                             attn_sinks[l], bias_tab, gnorm, l)
        h = _out_proj(h, modl, oa.reshape(T, ATTN_W), og.reshape(T, GV_W), w_out_b, l, S)
        h = _ffn(h, modl, n2, w21, w23, w22, fmod, fnw, l, 6, S, final=(l == L - 1))
    return h.reshape(B, S, D)
```

```python
import functools

import jax
import jax.numpy as jnp
import numpy as np
from jax import lax
from jax.experimental import pallas as pl
from jax.experimental.pallas import tpu as pltpu

F32 = jnp.float32
BF16 = jnp.bfloat16

ATTN_HD = 64
N_Q_HEADS = 16
N_KV_HEADS = 2
GQA_GROUP = N_Q_HEADS // N_KV_HEADS
PAIRS_PER_KV = GQA_GROUP // 2
WINDOW = 128
BLOCK = 128
GLA_HEADS = 4
GLA_DK = 128
GLA_DV = 256
GLA_RANK = 16
GLA_TAU = 16.0
GLA_CHUNK = 64
N_MOD = 9
EPS = 1e-6
MASK_VALUE = -1e30

LANES = 128
SUBLANES = 8
VMEM_LIMIT_BYTES = 56 * 1024 * 1024

FFN_TM = 1024
FFN_TF = 512
FFN_NORM_START = 3
FFN_NORM_SLICES = 8
FFN_NORM_ROWS = FFN_TM // FFN_NORM_SLICES
MIX_TM = 512
PROJ_PIECE = 256
OUT_TM = 512
MOD_TN = 1024

ATTN_W = N_Q_HEADS * ATTN_HD
KV_DUP_W = N_KV_HEADS * 2 * ATTN_HD
GK_W = GLA_HEADS * GLA_DK
GV_W = GLA_HEADS * GLA_DV
_SEG = {}
_off = 0
for _name, _w in (("qa", ATTN_W), ("kv", 2 * N_KV_HEADS * ATTN_HD), ("qg", GK_W), ("kg", GK_W),
                  ("vg", GV_W), ("rg", GV_W), ("gate", GLA_RANK)):
    _SEG[_name] = (_off, _w)
    _off += _w
PROJ_W = _off


def _params(*sem):
    return pltpu.CompilerParams(dimension_semantics=sem, vmem_limit_bytes=VMEM_LIMIT_BYTES)


def _rms_mod(x, nw, scale, shift):
    y = x * lax.rsqrt(jnp.mean(x * x, axis=-1, keepdims=True) + EPS) * nw
    return y * (1.0 + scale) + shift


def _silu(x):
    return x * jax.nn.sigmoid(x)


def _mod_kernel(c_ref, w_ref, b_ref, o_ref):
    ca = _silu(c_ref[...]).astype(BF16)
    o_ref[...] = jnp.dot(ca, w_ref[...].astype(BF16), preferred_element_type=F32) + b_ref[...]


def _modulation(c_pad, w, b):
    L, D, N = w.shape
    rows = c_pad.shape[0]
    return pl.pallas_call(
        _mod_kernel,
        grid=(L, N // MOD_TN),
        in_specs=[
            pl.BlockSpec((rows, D), lambda l, j: (0, 0)),
            pl.BlockSpec((None, D, MOD_TN), lambda l, j: (l, 0, j)),
            pl.BlockSpec((None, 1, MOD_TN), lambda l, j: (l, 0, j)),
        ],
        out_specs=pl.BlockSpec((None, rows, MOD_TN), lambda l, j: (l, 0, j)),
        out_shape=jax.ShapeDtypeStruct((L, rows, N), F32),
        compiler_params=_params("parallel", "parallel"),
        name="adaln_modulation",
    )(c_pad, w, b.reshape(L, 1, N))


def _ffn_kernel(h_hbm, modc_ref, modn_ref, nw_ref, w1_ref, w3_ref, w2_ref, fmod_ref, fnw_ref,
                o_hbm, hbuf, acc_ref, u0_ref, u1_ref, sem_in, sem_out, *, row0, nf, nt, final):
    i = pl.program_id(0)
    f = pl.program_id(1)
    p = i & 1
    q = 1 - p

    def tile_rows(t):
        return pl.ds(pl.multiple_of(t * FFN_TM, FFN_TM), FFN_TM)

    def read(t, slot):
        return pltpu.make_async_copy(h_hbm.at[tile_rows(t), :], hbuf.at[slot], sem_in.at[slot])

    def write(t, slot):
        return pltpu.make_async_copy(hbuf.at[slot], o_hbm.at[tile_rows(t), :], sem_out.at[slot])

    def normed(x, mod_ref):
        u = _rms_mod(x, nw_ref[...], mod_ref[row0 + 1:row0 + 2, :], mod_ref[row0:row0 + 1, :])
        return u.astype(BF16)

    @pl.when(jnp.logical_and(i == 0, f == 0))
    def _():
        read(0, 0).start()
        read(0, 0).wait()
        u0_ref[...] = normed(hbuf[0], modc_ref)

    @pl.when(f == 0)
    def _():
        acc_ref[...] = jnp.zeros_like(acc_ref)

    @pl.when(f == 1)
    def _():
        @pl.when(i >= 1)
        def _():
            write(i - 1, q).wait()

        @pl.when(i + 1 < nt)
        def _():
            read(i + 1, q).start()

    @pl.when(jnp.logical_and(f == FFN_NORM_START, i + 1 < nt))
    def _():
        read(i + 1, q).wait()

    def step(u_cur, u_nxt):
        src = jnp.where(f < FFN_NORM_START, p, q)
        sl = jnp.clip(f - FFN_NORM_START, 0, FFN_NORM_SLICES - 1)
        rows = pl.ds(pl.multiple_of(sl * FFN_NORM_ROWS, FFN_NORM_ROWS), FFN_NORM_ROWS)
        u_nxt[rows, :] = normed(hbuf[src, rows, :], modn_ref)
        u = u_cur[...]
        a = jnp.dot(u, w1_ref[...], preferred_element_type=F32)
        b = jnp.dot(u, w3_ref[...], preferred_element_type=F32)
        hm = (_silu(a) * b).astype(BF16)
        acc_ref[...] += jnp.dot(hm, w2_ref[...], preferred_element_type=F32)

    @pl.when(p == 0)
    def _():
        step(u0_ref, u1_ref)

    @pl.when(p == 1)
    def _():
        step(u1_ref, u0_ref)

    @pl.when(f == nf - 1)
    def _():
        y = hbuf[p] + (0.5 * modc_ref[row0 + 2:row0 + 3, :]) * acc_ref[...]
        if final:
            y = _rms_mod(y, fnw_ref[...], fmod_ref[1:2, :], fmod_ref[0:1, :])
        hbuf[p] = y
        write(i, p).start()

        @pl.when(i == nt - 1)
        def _():
            write(i, p).wait()


def _ffn(h, modl, nw, w1, w3, w2, fmod, fnw, l, row0, seq, final=False):
    T, D = h.shape
    F = w1.shape[2]
    nf = F // FFN_TF
    nt = T // FFN_TM
    tpb = seq // FFN_TM
    assert nf == FFN_NORM_START + FFN_NORM_SLICES
    nxt = lambda i: jnp.minimum(i + 1, nt - 1)
    return pl.pallas_call(
        functools.partial(_ffn_kernel, row0=row0, nf=nf, nt=nt, final=final),
        grid=(nt, nf),
        in_specs=[
            pl.BlockSpec(memory_space=pl.ANY),
            pl.BlockSpec((None, N_MOD, D), lambda i, f: (i // tpb, 0, 0)),
            pl.BlockSpec((None, N_MOD, D), lambda i, f: (nxt(i) // tpb, 0, 0)),
            pl.BlockSpec((None, 1, D), lambda i, f: (l, 0, 0)),
            pl.BlockSpec((None, D, FFN_TF), lambda i, f: (l, 0, f)),
            pl.BlockSpec((None, D, FFN_TF), lambda i, f: (l, 0, f)),
            pl.BlockSpec((None, FFN_TF, D), lambda i, f: (l, f, 0)),
            pl.BlockSpec((None, 2, D), lambda i, f: (i // tpb, 0, 0)),
            pl.BlockSpec((1, D), lambda i, f: (0, 0)),
        ],
        out_specs=pl.BlockSpec(memory_space=pl.ANY),
        out_shape=jax.ShapeDtypeStruct((T, D), F32),
        scratch_shapes=[
            pltpu.VMEM((2, FFN_TM, D), F32),
            pltpu.VMEM((FFN_TM, D), F32),
            pltpu.VMEM((FFN_TM, D), BF16),
            pltpu.VMEM((FFN_TM, D), BF16),
            pltpu.SemaphoreType.DMA((2,)),
            pltpu.SemaphoreType.DMA((2,)),
        ],
        compiler_params=_params("arbitrary", "arbitrary"),
        name="swiglu_ffn",
    )(h, modl, modl, nw, w1, w3, w2, fmod, fnw)


def _log_sigmoid(x):
    return jnp.minimum(x, 0.0) - jnp.log1p(jnp.exp(-jnp.abs(x)))


_BUF_FIELDS = (("qa", ATTN_W, BF16), ("kd", KV_DUP_W, BF16), ("vd", KV_DUP_W, BF16), ("qg", GK_W, F32),
               ("kg", GK_W, F32), ("vg", GV_W, BF16), ("rg", GV_W, F32), ("gg", GK_W, F32))


def _project_pieces(h_ref, mod_ref, nw_ref, w_ref, wg_ref, gw_ref, gb_ref, buf):
    qa_ref, kd_ref, vd_ref, qg_ref, kg_ref, vg_ref, rg_ref, gg_ref = buf
    u = _rms_mod(h_ref[...], nw_ref[...], mod_ref[4:5, :], mod_ref[3:4, :]).astype(BF16)

    def piece(c0):
        return jnp.dot(u, w_ref[:, c0:c0 + PROJ_PIECE], preferred_element_type=F32)

    def plain(name, dst):
        start, width = _SEG[name]
        for c0 in range(0, width, PROJ_PIECE):
            dst[:, c0:c0 + PROJ_PIECE] = piece(start + c0).astype(dst.dtype)
            yield

    yield from plain("qa", qa_ref)
    kv = piece(_SEG["kv"][0])
    lo = lax.broadcasted_iota(jnp.int32, (kv.shape[0], LANES), 1) < ATTN_HD
    for src, dst in ((kv[:, :LANES], kd_ref), (kv[:, LANES:], vd_ref)):
        swapped = pltpu.roll(src, ATTN_HD, axis=1)
        dst[:, :LANES] = jnp.where(lo, src, swapped).astype(BF16)
        dst[:, LANES:] = jnp.where(lo, swapped, src).astype(BF16)
    yield
    yield from plain("qg", qg_ref)
    yield from plain("kg", kg_ref)
    yield from plain("vg", vg_ref)
    yield from plain("rg", rg_ref)
    gate_lr = jnp.dot(u, wg_ref[...], preferred_element_type=F32).astype(BF16)
    z = jnp.dot(gate_lr, gw_ref[...], preferred_element_type=F32) + gb_ref[...]
    gg_ref[...] = _log_sigmoid(z) / GLA_TAU
    yield


def _attn_bias_table():
    qi = np.arange(BLOCK)[:, None]
    kj = np.arange(2 * BLOCK)[None, :]
    dist = qi + BLOCK - kj
    band = (dist >= 0) & (dist < WINDOW)
    slopes = np.exp2(-8.0 * np.arange(1, N_Q_HEADS + 1, dtype=np.float32) / N_Q_HEADS).astype(np.float32)
    tab = np.empty((N_KV_HEADS, PAIRS_PER_KV * BLOCK, 4 * BLOCK), np.float32)
    for kh in range(N_KV_HEADS):
        for p in range(PAIRS_PER_KV):
            for e in range(2):
                hd = kh * GQA_GROUP + 2 * p + e
                bias = np.where(band, -slopes[hd] * dist.astype(np.float32), np.float32(MASK_VALUE))
                tab[kh, p * BLOCK:(p + 1) * BLOCK, e * 2 * BLOCK:(e + 1) * 2 * BLOCK] = bias
    return tab


def _attention_stage(buf, tk_ref, tv_ref, sink_ref, bias_ref, first, o_ref):
    q_ref, kc_ref, vc_ref = buf[0], buf[1], buf[2]
    tm = q_ref.shape[0]
    lane = lax.broadcasted_iota(jnp.int32, (2 * BLOCK, LANES), 1)
    lo = lane < ATTN_HD
    col = lax.broadcasted_iota(jnp.int32, (1, 4 * BLOCK), 1)
    prev_cols = (col & (2 * BLOCK - 1)) < BLOCK
    first_mask = jnp.where(jnp.logical_and(first, prev_cols), MASK_VALUE, 0.0)
    for j in range(tm // BLOCK):
        rows = slice(j * BLOCK, (j + 1) * BLOCK)
        for kh in range(N_KV_HEADS):
            cols = slice(kh * LANES, (kh + 1) * LANES)
            if j == 0:
                kprev, vprev = tk_ref[:, cols], tv_ref[:, cols]
            else:
                prows = slice((j - 1) * BLOCK, j * BLOCK)
                kprev, vprev = kc_ref[prows, cols], vc_ref[prows, cols]
            kk = jnp.concatenate([kprev, kc_ref[rows, cols]], axis=0)
            vv = jnp.concatenate([vprev, vc_ref[rows, cols]], axis=0)
            zero = jnp.zeros_like(kk)
            kbd = jnp.concatenate([jnp.where(lo, kk, zero), jnp.where(lo, zero, kk)], axis=0)
            vbd = jnp.concatenate([jnp.where(lo, vv, zero), jnp.where(lo, zero, vv)], axis=0)
            qs = jnp.concatenate(
                [q_ref[rows, (kh * PAIRS_PER_KV + p) * LANES:(kh * PAIRS_PER_KV + p + 1) * LANES]
                 for p in range(PAIRS_PER_KV)], axis=0)
            s = lax.dot_general(qs, kbd, (((1,), (1,)), ((), ())), preferred_element_type=F32)
            s = s * (ATTN_HD ** -0.5) + bias_ref[kh]
            if j == 0:
                s = s + first_mask
            prow = []
            for p in range(PAIRS_PER_KV):
                pe = []
                for e in range(2):
                    sink = sink_ref[kh * GQA_GROUP + 2 * p + e]
                    t = s[p * BLOCK:(p + 1) * BLOCK, e * 2 * BLOCK:(e + 1) * 2 * BLOCK]
                    m = jnp.maximum(jnp.max(t, axis=-1, keepdims=True), sink)
                    pexp = jnp.exp(t - m)
                    den = jnp.sum(pexp, axis=-1, keepdims=True) + jnp.exp(sink - m)
                    pe.append((pexp / den).astype(BF16))
                prow.append(jnp.concatenate(pe, axis=1))
            pmat = jnp.concatenate(prow, axis=0)
            o = jnp.dot(pmat, vbd, preferred_element_type=F32)
            for p in range(PAIRS_PER_KV):
                c0 = (kh * PAIRS_PER_KV + p) * LANES
                o_ref[rows, c0:c0 + LANES] = o[p * BLOCK:(p + 1) * BLOCK].astype(BF16)
            yield
    tk_ref[...] = kc_ref[tm - BLOCK:tm, :]
    tv_ref[...] = vc_ref[tm - BLOCK:tm, :]


def _gla_stage(buf, nw_ref, st_ref, o_ref):
    q_ref, k_ref, v_ref, r_ref, g_ref = buf[3], buf[4], buf[5], buf[6], buf[7]
    tm = q_ref.shape[0]
    ci = lax.broadcasted_iota(jnp.int32, (GLA_CHUNK, GLA_CHUNK), 0)
    si = lax.broadcasted_iota(jnp.int32, (GLA_CHUNK, GLA_CHUNK), 1)
    causal = si <= ci
    tri = causal.astype(BF16)
    nw = nw_ref[...]
    for c in range(tm // GLA_CHUNK):
        rows = slice(c * GLA_CHUNK, (c + 1) * GLA_CHUNK)
        g = g_ref[rows, :]
        g_hi = g.astype(BF16)
        g_r1 = g - g_hi.astype(F32)
        g_mid = g_r1.astype(BF16)
        g_lo = (g_r1 - g_mid.astype(F32)).astype(BF16)
        b = (jnp.dot(tri, g_hi, preferred_element_type=F32)
             + jnp.dot(tri, g_mid, preferred_element_type=F32)
             + jnp.dot(tri, g_lo, preferred_element_type=F32))
        b_last = b[GLA_CHUNK - 1:GLA_CHUNK, :]
        q = q_ref[rows, :] * (GLA_DK ** -0.5)
        k = k_ref[rows, :]
        q_t = (q * jnp.exp(b)).astype(BF16)
        k_t = (k * jnp.exp(-b)).astype(BF16)
        k_end = (k * jnp.exp(b_last - b)).astype(BF16)
        decay = jnp.exp(b_last)
        for hd in range(GLA_HEADS):
            kc = slice(hd * GLA_DK, (hd + 1) * GLA_DK)
            vc = slice(hd * GLA_DV, (hd + 1) * GLA_DV)
            v = v_ref[rows, vc]
            a = lax.dot_general(q_t[:, kc], k_t[:, kc], (((1,), (1,)), ((), ())),
                                preferred_element_type=F32)
            a = jnp.where(causal, a, 0.0).astype(BF16)
            st = st_ref[hd]
            o = jnp.dot(a, v, preferred_element_type=F32)
            o = o + lax.dot_general(q_t[:, kc], st.astype(BF16), (((1,), (1,)), ((), ())),
                                    preferred_element_type=F32)
            upd = lax.dot_general(v, k_end[:, kc], (((0,), (0,)), ((), ())),
                                  preferred_element_type=F32)
            st_ref[hd] = st * decay[:, kc] + upd
            o = o * lax.rsqrt(jnp.mean(o * o, axis=-1, keepdims=True) + EPS) * nw
            o_ref[rows, vc] = (o * _silu(r_ref[rows, vc])).astype(BF16)
        yield


def _mixer_kernel(sink_ref, h_ref, mod_ref, nw_ref, w_ref, wg_ref, gw_ref, gb_ref, bias_ref, gnw_ref,
                  oa_ref, og_ref, *scratch):
    nb = len(_BUF_FIELDS)
    bufs = (scratch[:nb], scratch[nb:2 * nb])
    tk_ref, tv_ref, st_ref = scratch[2 * nb:]
    s = pl.program_id(1)

    @pl.when(s == 0)
    def _():
        for r in bufs[1] + (tk_ref, tv_ref, st_ref):
            r[...] = jnp.zeros_like(r)

    def step(project_into, mix_from):
        pieces = _project_pieces(h_ref, mod_ref, nw_ref, w_ref, wg_ref, gw_ref, gb_ref, project_into)
        next(pieces)
        for _ in _attention_stage(mix_from, tk_ref, tv_ref, sink_ref, bias_ref, s == 1, oa_ref):
            next(pieces, None)
        for _ in _gla_stage(mix_from, gnw_ref, st_ref, og_ref):
            next(pieces, None)
        for _ in pieces:
            pass

    @pl.when((s & 1) == 0)
    def _():
        step(bufs[0], bufs[1])

    @pl.when((s & 1) == 1)
    def _():
        step(bufs[1], bufs[0])


def _mixer_core(h3, modl, nw, w_in_b, w_gate_p, gate_w2_p, gate_b, sinks, bias_tab, gnorm, l):
    B, S, D = h3.shape
    nt = S // MIX_TM
    single = pl.Buffered(1)
    proj_tile = lambda b, s: (b, jnp.minimum(s, nt - 1), 0)
    mix_tile = lambda b, s: (b, jnp.maximum(s - 1, 0), 0)
    scratch = [pltpu.VMEM((MIX_TM, w), dt) for _ in range(2) for _, w, dt in _BUF_FIELDS]
    scratch += [pltpu.VMEM((BLOCK, KV_DUP_W), BF16), pltpu.VMEM((BLOCK, KV_DUP_W), BF16),
                pltpu.VMEM((GLA_HEADS, GLA_DV, GLA_DK), F32)]
    return pl.pallas_call(
        _mixer_kernel,
        grid=(B, nt + 1),
        in_specs=[
            pl.BlockSpec(memory_space=pltpu.SMEM),
            pl.BlockSpec((None, MIX_TM, D), proj_tile),
            pl.BlockSpec((None, N_MOD, D), lambda b, s: (b, 0, 0)),
            pl.BlockSpec((None, 1, D), lambda b, s: (l, 0, 0)),
            pl.BlockSpec((None, D, PROJ_W), lambda b, s: (l, 0, 0), pipeline_mode=single),
            pl.BlockSpec((None, D, LANES), lambda b, s: (l, 0, 0), pipeline_mode=single),
            pl.BlockSpec((None, LANES, GK_W), lambda b, s: (l, 0, 0)),
            pl.BlockSpec((None, 1, GK_W), lambda b, s: (l, 0, 0)),
            pl.BlockSpec(bias_tab.shape, lambda b, s: (0, 0, 0), pipeline_mode=single),
            pl.BlockSpec((None, 1, GLA_DV), lambda b, s: (l, 0, 0)),
        ],
        out_specs=[pl.BlockSpec((None, MIX_TM, ATTN_W), mix_tile),
                   pl.BlockSpec((None, MIX_TM, GV_W), mix_tile)],
        out_shape=[jax.ShapeDtypeStruct((B, S, ATTN_W), BF16), jax.ShapeDtypeStruct((B, S, GV_W), BF16)],
        scratch_shapes=scratch,
        compiler_params=_params("parallel", "arbitrary"),
        name="mixer_core",
    )(sinks, h3, modl, nw, w_in_b, w_gate_p, gate_w2_p, gate_b, bias_tab, gnorm)


def _out_kernel(h_ref, mod_ref, oa_ref, og_ref, w_ref, o_ref):
    ka = oa_ref.shape[1]
    y = jnp.dot(oa_ref[...], w_ref[:ka, :], preferred_element_type=F32)
    y = y + jnp.dot(og_ref[...], w_ref[ka:, :], preferred_element_type=F32)
    o_ref[...] = h_ref[...] + mod_ref[5:6, :] * y


def _out_proj(h, modl, oa, og, w_out, l, seq):
    T, D = h.shape
    tpb = seq // OUT_TM
    return pl.pallas_call(
        _out_kernel,
        grid=(T // OUT_TM,),
        in_specs=[
            pl.BlockSpec((OUT_TM, D), lambda i: (i, 0)),
            pl.BlockSpec((None, N_MOD, D), lambda i: (i // tpb, 0, 0)),
            pl.BlockSpec((OUT_TM, oa.shape[1]), lambda i: (i, 0)),
            pl.BlockSpec((OUT_TM, og.shape[1]), lambda i: (i, 0)),
            pl.BlockSpec((None,) + w_out.shape[1:], lambda i: (l, 0, 0)),
        ],
        out_specs=pl.BlockSpec((OUT_TM, D), lambda i: (i, 0)),
        out_shape=jax.ShapeDtypeStruct((T, D), F32),
        compiler_params=_params("parallel"),
        name="mixer_out_proj",
    )(h, modl, oa, og, w_out)


def kernel(x, c, ada_w, ada_b, norm_ffn1, ffn1_w1, ffn1_w3, ffn1_w2, norm_mix, w_in, gla_gate_w2,
           gla_gate_b, attn_sinks, gla_norm, w_out, norm_ffn2, ffn2_w1, ffn2_w3, ffn2_w2,
           final_ada_w, final_ada_b, final_norm):
    B, S, D = x.shape
    L = ada_w.shape[0]
    T = B * S

    c_pad = jnp.pad(c, ((0, SUBLANES - B), (0, 0)))
    mod = _modulation(c_pad, ada_w, ada_b)[:, :B].reshape(L, B, N_MOD, D)
    fmod = _modulation(c_pad, final_ada_w[None], final_ada_b[None])[0, :B].reshape(B, 2, D)

    w11, w13, w12 = ffn1_w1.astype(BF16), ffn1_w3.astype(BF16), ffn1_w2.astype(BF16)
    w21, w23, w22 = ffn2_w1.astype(BF16), ffn2_w3.astype(BF16), ffn2_w2.astype(BF16)
    assert w_in.shape[-1] == PROJ_W
    w_in_b = w_in.astype(BF16)
    gate0 = _SEG["gate"][0]
    w_gate_p = jnp.pad(w_in[:, :, gate0:], ((0, 0), (0, 0), (0, LANES - GLA_RANK))).astype(BF16)
    w_out_b = w_out.astype(BF16)
    gate_w2_p = jnp.pad(gla_gate_w2, ((0, 0), (0, LANES - GLA_RANK), (0, 0))).astype(BF16)
    gate_b = gla_gate_b.reshape(L, 1, GK_W)
    n1, nm, n2 = (t.reshape(L, 1, D) for t in (norm_ffn1, norm_mix, norm_ffn2))
    gnorm = gla_norm.reshape(L, 1, GLA_DV)
    bias_tab = jnp.asarray(_attn_bias_table())

    fnw = final_norm.reshape(1, D)
    h = x.reshape(T, D)
    for l in range(L):
        modl = mod[l]
        h = _ffn(h, modl, n1, w11, w13, w12, fmod, fnw, l, 0, S)
        oa, og = _mixer_core(h.reshape(B, S, D), modl, nm, w_in_b, w_gate_p, gate_w2_p, gate_b,
                             attn_sinks[l], bias_tab, gnorm, l)
        h = _out_proj(h, modl, oa.reshape(T, ATTN_W), og.reshape(T, GV_W), w_out_b, l, S)
        h = _ffn(h, modl, n2, w21, w23, w22, fmod, fnw, l, 6, S, final=(l == L - 1))
    return h.reshape(B, S, D)
```

```python
import functools

import jax
import jax.numpy as jnp
import numpy as np
from jax import lax
from jax.experimental import pallas as pl
from jax.experimental.pallas import tpu as pltpu

F32 = jnp.float32
BF16 = jnp.bfloat16

ATTN_HD = 64
N_Q_HEADS = 16
N_KV_HEADS = 2
GQA_GROUP = N_Q_HEADS // N_KV_HEADS
PAIRS_PER_KV = GQA_GROUP // 2
WINDOW = 128
BLOCK = 128
GLA_HEADS = 4
GLA_DK = 128
GLA_DV = 256
GLA_RANK = 16
GLA_TAU = 16.0
GLA_CHUNK = 64
N_MOD = 9
EPS = 1e-6
MASK_VALUE = -1e30

LANES = 128
SUBLANES = 8
VMEM_LIMIT_BYTES = 56 * 1024 * 1024

FFN_TM = 1024
FFN_TF = 512
FFN_NORM_START = 3
FFN_NORM_SLICES = 8
FFN_NORM_ROWS = FFN_TM // FFN_NORM_SLICES
MIX_TM = 512
PROJ_PIECE = 256
OUT_TM = 512
MOD_TN = 1024
CAST_ROWS = 512

ATTN_W = N_Q_HEADS * ATTN_HD
KV_DUP_W = N_KV_HEADS * 2 * ATTN_HD
GK_W = GLA_HEADS * GLA_DK
GV_W = GLA_HEADS * GLA_DV
_SEG = {}
_off = 0
for _name, _w in (("qa", ATTN_W), ("kv", 2 * N_KV_HEADS * ATTN_HD), ("qg", GK_W), ("kg", GK_W),
                  ("vg", GV_W), ("rg", GV_W), ("gate", GLA_RANK)):
    _SEG[_name] = (_off, _w)
    _off += _w
PROJ_W = _off


def _params(*sem):
    return pltpu.CompilerParams(dimension_semantics=sem, vmem_limit_bytes=VMEM_LIMIT_BYTES)


def _rms_mod(x, nw, scale, shift):
    y = x * lax.rsqrt(jnp.mean(x * x, axis=-1, keepdims=True) + EPS) * nw
    return y * (1.0 + scale) + shift


def _silu(x):
    return x * jax.nn.sigmoid(x)


def _cast_kernel(x_ref, o_ref):
    o_ref[...] = x_ref[...].astype(BF16)


def _to_bf16(w):
    L, R, C = w.shape
    return pl.pallas_call(
        _cast_kernel,
        grid=(L, R // CAST_ROWS),
        in_specs=[pl.BlockSpec((None, CAST_ROWS, C), lambda l, r: (l, r, 0))],
        out_specs=pl.BlockSpec((None, CAST_ROWS, C), lambda l, r: (l, r, 0)),
        out_shape=jax.ShapeDtypeStruct(w.shape, BF16),
        compiler_params=_params("parallel", "parallel"),
        name="weight_to_bf16",
    )(w)


def _mod_kernel(c_ref, w_ref, b_ref, o_ref):
    ca = _silu(c_ref[...]).astype(BF16)
    o_ref[...] = jnp.dot(ca, w_ref[...].astype(BF16), preferred_element_type=F32) + b_ref[...]


def _modulation(c_pad, w, b):
    L, D, N = w.shape
    rows = c_pad.shape[0]
    return pl.pallas_call(
        _mod_kernel,
        grid=(L, N // MOD_TN),
        in_specs=[
            pl.BlockSpec((rows, D), lambda l, j: (0, 0)),
            pl.BlockSpec((None, D, MOD_TN), lambda l, j: (l, 0, j)),
            pl.BlockSpec((None, 1, MOD_TN), lambda l, j: (l, 0, j)),
        ],
        out_specs=pl.BlockSpec((None, rows, MOD_TN), lambda l, j: (l, 0, j)),
        out_shape=jax.ShapeDtypeStruct((L, rows, N), F32),
        compiler_params=_params("parallel", "parallel"),
        name="adaln_modulation",
    )(c_pad, w, b.reshape(L, 1, N))


def _ffn_kernel(h_hbm, modc_ref, modn_ref, nw_ref, w1_ref, w3_ref, w2_ref, fmod_ref, fnw_ref,
                o_hbm, hbuf, acc_ref, u0_ref, u1_ref, sem_in, sem_out, *, row0, nf, nt, final):
    i = pl.program_id(0)
    f = pl.program_id(1)
    p = i & 1
    q = 1 - p

    def tile_rows(t):
        return pl.ds(pl.multiple_of(t * FFN_TM, FFN_TM), FFN_TM)

    def read(t, slot):
        return pltpu.make_async_copy(h_hbm.at[tile_rows(t), :], hbuf.at[slot], sem_in.at[slot])

    def write(t, slot):
        return pltpu.make_async_copy(hbuf.at[slot], o_hbm.at[tile_rows(t), :], sem_out.at[slot])

    def normed(x, mod_ref):
        u = _rms_mod(x, nw_ref[...], mod_ref[row0 + 1:row0 + 2, :], mod_ref[row0:row0 + 1, :])
        return u.astype(BF16)

    @pl.when(jnp.logical_and(i == 0, f == 0))
    def _():
        read(0, 0).start()
        read(0, 0).wait()
        u0_ref[...] = normed(hbuf[0], modc_ref)

    @pl.when(f == 1)
    def _():
        @pl.when(i >= 1)
        def _():
            write(i - 1, q).wait()

        @pl.when(i + 1 < nt)
        def _():
            read(i + 1, q).start()

    @pl.when(jnp.logical_and(f == FFN_NORM_START, i + 1 < nt))
    def _():
        read(i + 1, q).wait()

    def step(slot, u_cur, u_nxt, first, last):
        src = jnp.where(f < FFN_NORM_START, p, q)
        sl = jnp.clip(f - FFN_NORM_START, 0, FFN_NORM_SLICES - 1)
        rows = pl.ds(pl.multiple_of(sl * FFN_NORM_ROWS, FFN_NORM_ROWS), FFN_NORM_ROWS)
        u_nxt[rows, :] = normed(hbuf[src, rows, :], modn_ref)
        u = u_cur[...]
        a = jnp.dot(u, w1_ref[...], preferred_element_type=F32)
        b = jnp.dot(u, w3_ref[...], preferred_element_type=F32)
        hm = (_silu(a) * b).astype(BF16)
        total = jnp.dot(hm, w2_ref[...], preferred_element_type=F32)
        if not first:
            total = acc_ref[...] + total
        if not last:
            acc_ref[...] = total
            return
        y = hbuf[slot] + (0.5 * modc_ref[row0 + 2:row0 + 3, :]) * total
        if final:
            y = _rms_mod(y, fnw_ref[...], fmod_ref[1:2, :], fmod_ref[0:1, :])
        hbuf[slot] = y

    for slot, (u_cur, u_nxt) in enumerate(((u0_ref, u1_ref), (u1_ref, u0_ref))):
        in_slot = p == slot
        pl.when(jnp.logical_and(in_slot, f == 0))(
            functools.partial(step, slot, u_cur, u_nxt, True, False))
        pl.when(jnp.logical_and(in_slot, jnp.logical_and(f > 0, f < nf - 1)))(
            functools.partial(step, slot, u_cur, u_nxt, False, False))
        pl.when(jnp.logical_and(in_slot, f == nf - 1))(
            functools.partial(step, slot, u_cur, u_nxt, False, True))

    @pl.when(f == nf - 1)
    def _():
        write(i, p).start()

        @pl.when(i == nt - 1)
        def _():
            write(i, p).wait()


def _ffn(h, modl, nw, w1, w3, w2, fmod, fnw, l, row0, seq, final=False):
    T, D = h.shape
    F = w1.shape[2]
    nf = F // FFN_TF
    nt = T // FFN_TM
    tpb = seq // FFN_TM
    assert nf == FFN_NORM_START + FFN_NORM_SLICES
    nxt = lambda i: jnp.minimum(i + 1, nt - 1)
    return pl.pallas_call(
        functools.partial(_ffn_kernel, row0=row0, nf=nf, nt=nt, final=final),
        grid=(nt, nf),
        in_specs=[
            pl.BlockSpec(memory_space=pl.ANY),
            pl.BlockSpec((None, N_MOD, D), lambda i, f: (i // tpb, 0, 0)),
            pl.BlockSpec((None, N_MOD, D), lambda i, f: (nxt(i) // tpb, 0, 0)),
            pl.BlockSpec((None, 1, D), lambda i, f: (l, 0, 0)),
            pl.BlockSpec((None, D, FFN_TF), lambda i, f: (l, 0, f)),
            pl.BlockSpec((None, D, FFN_TF), lambda i, f: (l, 0, f)),
            pl.BlockSpec((None, FFN_TF, D), lambda i, f: (l, f, 0)),
            pl.BlockSpec((None, 2, D), lambda i, f: (i // tpb, 0, 0)),
            pl.BlockSpec((1, D), lambda i, f: (0, 0)),
        ],
        out_specs=pl.BlockSpec(memory_space=pl.ANY),
        out_shape=jax.ShapeDtypeStruct((T, D), F32),
        scratch_shapes=[
            pltpu.VMEM((2, FFN_TM, D), F32),
            pltpu.VMEM((FFN_TM, D), F32),
            pltpu.VMEM((FFN_TM, D), BF16),
            pltpu.VMEM((FFN_TM, D), BF16),
            pltpu.SemaphoreType.DMA((2,)),
            pltpu.SemaphoreType.DMA((2,)),
        ],
        compiler_params=_params("arbitrary", "arbitrary"),
        name="swiglu_ffn",
    )(h, modl, modl, nw, w1, w3, w2, fmod, fnw)


def _log_sigmoid(x):
    return jnp.minimum(x, 0.0) - jnp.log1p(jnp.exp(-jnp.abs(x)))


_BUF_FIELDS = (("qa", ATTN_W, BF16), ("kd", KV_DUP_W, BF16), ("vd", KV_DUP_W, BF16), ("qg", GK_W, F32),
               ("kg", GK_W, F32), ("vg", GV_W, BF16), ("rg", GV_W, F32), ("gg", GK_W, F32))


def _project_pieces(h_ref, mod_ref, nw_ref, w_ref, wg_ref, gw_ref, gb_ref, buf):
    qa_ref, kd_ref, vd_ref, qg_ref, kg_ref, vg_ref, rg_ref, gg_ref = buf
    u = _rms_mod(h_ref[...], nw_ref[...], mod_ref[4:5, :], mod_ref[3:4, :]).astype(BF16)

    def piece(c0):
        return jnp.dot(u, w_ref[:, c0:c0 + PROJ_PIECE], preferred_element_type=F32)

    def plain(name, dst):
        start, width = _SEG[name]
        for c0 in range(0, width, PROJ_PIECE):
            dst[:, c0:c0 + PROJ_PIECE] = piece(start + c0).astype(dst.dtype)
            yield

    yield from plain("qa", qa_ref)
    kv = piece(_SEG["kv"][0])
    lo = lax.broadcasted_iota(jnp.int32, (kv.shape[0], LANES), 1) < ATTN_HD
    for src, dst in ((kv[:, :LANES], kd_ref), (kv[:, LANES:], vd_ref)):
        swapped = pltpu.roll(src, ATTN_HD, axis=1)
        dst[:, :LANES] = jnp.where(lo, src, swapped).astype(BF16)
        dst[:, LANES:] = jnp.where(lo, swapped, src).astype(BF16)
    yield
    yield from plain("qg", qg_ref)
    yield from plain("kg", kg_ref)
    yield from plain("vg", vg_ref)
    yield from plain("rg", rg_ref)
    gate_lr = jnp.dot(u, wg_ref[...], preferred_element_type=F32).astype(BF16)
    z = jnp.dot(gate_lr, gw_ref[...], preferred_element_type=F32) + gb_ref[...]
    gg_ref[...] = _log_sigmoid(z) / GLA_TAU
    yield


def _attn_bias_table():
    qi = np.arange(BLOCK)[:, None]
    kj = np.arange(2 * BLOCK)[None, :]
    dist = qi + BLOCK - kj
    band = (dist >= 0) & (dist < WINDOW)
    slopes = np.exp2(-8.0 * np.arange(1, N_Q_HEADS + 1, dtype=np.float32) / N_Q_HEADS).astype(np.float32)
    tab = np.empty((N_KV_HEADS, PAIRS_PER_KV * BLOCK, 4 * BLOCK), np.float32)
    for kh in range(N_KV_HEADS):
        for p in range(PAIRS_PER_KV):
            for e in range(2):
                hd = kh * GQA_GROUP + 2 * p + e
                bias = np.where(band, -slopes[hd] * dist.astype(np.float32), np.float32(MASK_VALUE))
                tab[kh, p * BLOCK:(p + 1) * BLOCK, e * 2 * BLOCK:(e + 1) * 2 * BLOCK] = bias
    return tab


def _attention_stage(buf, tk_ref, tv_ref, sink_ref, bias_ref, first, o_ref):
    q_ref, kc_ref, vc_ref = buf[0], buf[1], buf[2]
    tm = q_ref.shape[0]
    lane = lax.broadcasted_iota(jnp.int32, (2 * BLOCK, LANES), 1)
    lo = lane < ATTN_HD
    col = lax.broadcasted_iota(jnp.int32, (1, 4 * BLOCK), 1)
    prev_cols = (col & (2 * BLOCK - 1)) < BLOCK
    first_mask = jnp.where(jnp.logical_and(first, prev_cols), MASK_VALUE, 0.0)
    for j in range(tm // BLOCK):
        rows = slice(j * BLOCK, (j + 1) * BLOCK)
        for kh in range(N_KV_HEADS):
            cols = slice(kh * LANES, (kh + 1) * LANES)
            if j == 0:
                kprev, vprev = tk_ref[:, cols], tv_ref[:, cols]
            else:
                prows = slice((j - 1) * BLOCK, j * BLOCK)
                kprev, vprev = kc_ref[prows, cols], vc_ref[prows, cols]
            kk = jnp.concatenate([kprev, kc_ref[rows, cols]], axis=0)
            vv = jnp.concatenate([vprev, vc_ref[rows, cols]], axis=0)
            zero = jnp.zeros_like(kk)
            kbd = jnp.concatenate([jnp.where(lo, kk, zero), jnp.where(lo, zero, kk)], axis=0)
            vbd = jnp.concatenate([jnp.where(lo, vv, zero), jnp.where(lo, zero, vv)], axis=0)
            qs = jnp.concatenate(
                [q_ref[rows, (kh * PAIRS_PER_KV + p) * LANES:(kh * PAIRS_PER_KV + p + 1) * LANES]
                 for p in range(PAIRS_PER_KV)], axis=0)
            s = lax.dot_general(qs, kbd, (((1,), (1,)), ((), ())), preferred_element_type=F32)
            s = s * (ATTN_HD ** -0.5) + bias_ref[kh]
            if j == 0:
                s = s + first_mask
            prow = []
            for p in range(PAIRS_PER_KV):
                pe = []
                for e in range(2):
                    sink = sink_ref[kh * GQA_GROUP + 2 * p + e]
                    t = s[p * BLOCK:(p + 1) * BLOCK, e * 2 * BLOCK:(e + 1) * 2 * BLOCK]
                    m = jnp.maximum(jnp.max(t, axis=-1, keepdims=True), sink)
                    pexp = jnp.exp(t - m)
                    den = jnp.sum(pexp, axis=-1, keepdims=True) + jnp.exp(sink - m)
                    pe.append((pexp / den).astype(BF16))
                prow.append(jnp.concatenate(pe, axis=1))
            pmat = jnp.concatenate(prow, axis=0)
            o = jnp.dot(pmat, vbd, preferred_element_type=F32)
            for p in range(PAIRS_PER_KV):
                c0 = (kh * PAIRS_PER_KV + p) * LANES
                o_ref[rows, c0:c0 + LANES] = o[p * BLOCK:(p + 1) * BLOCK].astype(BF16)
            yield
    tk_ref[...] = kc_ref[tm - BLOCK:tm, :]
    tv_ref[...] = vc_ref[tm - BLOCK:tm, :]


def _gla_stage(buf, nw_ref, st_ref, o_ref):
    q_ref, k_ref, v_ref, r_ref, g_ref = buf[3], buf[4], buf[5], buf[6], buf[7]
    tm = q_ref.shape[0]
    ci = lax.broadcasted_iota(jnp.int32, (GLA_CHUNK, GLA_CHUNK), 0)
    si = lax.broadcasted_iota(jnp.int32, (GLA_CHUNK, GLA_CHUNK), 1)
    causal = si <= ci
    tri = causal.astype(BF16)
    nw = nw_ref[...]
    for c in range(tm // GLA_CHUNK):
        rows = slice(c * GLA_CHUNK, (c + 1) * GLA_CHUNK)
        g = g_ref[rows, :]
        g_hi = g.astype(BF16)
        g_r1 = g - g_hi.astype(F32)
        g_mid = g_r1.astype(BF16)
        g_lo = (g_r1 - g_mid.astype(F32)).astype(BF16)
        b = (jnp.dot(tri, g_hi, preferred_element_type=F32)
             + jnp.dot(tri, g_mid, preferred_element_type=F32)
             + jnp.dot(tri, g_lo, preferred_element_type=F32))
        b_last = b[GLA_CHUNK - 1:GLA_CHUNK, :]
        q = q_ref[rows, :] * (GLA_DK ** -0.5)
        k = k_ref[rows, :]
        q_t = (q * jnp.exp(b)).astype(BF16)
        k_t = (k * jnp.exp(-b)).astype(BF16)
        k_end = (k * jnp.exp(b_last - b)).astype(BF16)
        decay = jnp.exp(b_last)
        for hd in range(GLA_HEADS):
            kc = slice(hd * GLA_DK, (hd + 1) * GLA_DK)
            vc = slice(hd * GLA_DV, (hd + 1) * GLA_DV)
            v = v_ref[rows, vc]
            a = lax.dot_general(q_t[:, kc], k_t[:, kc], (((1,), (1,)), ((), ())),
                                preferred_element_type=F32)
            a = jnp.where(causal, a, 0.0).astype(BF16)
            st = st_ref[hd]
            o = jnp.dot(a, v, preferred_element_type=F32)
            o = o + lax.dot_general(q_t[:, kc], st.astype(BF16), (((1,), (1,)), ((), ())),
                                    preferred_element_type=F32)
            upd = lax.dot_general(v, k_end[:, kc], (((0,), (0,)), ((), ())),
                                  preferred_element_type=F32)
            st_ref[hd] = st * decay[:, kc] + upd
            o = o * lax.rsqrt(jnp.mean(o * o, axis=-1, keepdims=True) + EPS) * nw
            o_ref[rows, vc] = (o * _silu(r_ref[rows, vc])).astype(BF16)
        yield


def _mixer_kernel(sink_ref, h_ref, mod_ref, nw_ref, w_ref, wg_ref, gw_ref, gb_ref, bias_ref, gnw_ref,
                  oa_ref, og_ref, *scratch):
    nb = len(_BUF_FIELDS)
    bufs = (scratch[:nb], scratch[nb:2 * nb])
    tk_ref, tv_ref, st_ref = scratch[2 * nb:]
    s = pl.program_id(1)

    @pl.when(s == 0)
    def _():
        for r in bufs[1] + (tk_ref, tv_ref, st_ref):
            r[...] = jnp.zeros_like(r)

    def step(project_into, mix_from):
        pieces = _project_pieces(h_ref, mod_ref, nw_ref, w_ref, wg_ref, gw_ref, gb_ref, project_into)
        next(pieces)
        for _ in _attention_stage(mix_from, tk_ref, tv_ref, sink_ref, bias_ref, s == 1, oa_ref):
            next(pieces, None)
        for _ in _gla_stage(mix_from, gnw_ref, st_ref, og_ref):
            next(pieces, None)
        for _ in pieces:
            pass

    @pl.when((s & 1) == 0)
    def _():
        step(bufs[0], bufs[1])

    @pl.when((s & 1) == 1)
    def _():
        step(bufs[1], bufs[0])


def _mixer_core(h3, modl, nw, w_in_b, w_gate_p, gate_w2_p, gate_b, sinks, bias_tab, gnorm, l):
    B, S, D = h3.shape
    nt = S // MIX_TM
    single = pl.Buffered(1)
    proj_tile = lambda b, s: (b, jnp.minimum(s, nt - 1), 0)
    mix_tile = lambda b, s: (b, jnp.maximum(s - 1, 0), 0)
    scratch = [pltpu.VMEM((MIX_TM, w), dt) for _ in range(2) for _, w, dt in _BUF_FIELDS]
    scratch += [pltpu.VMEM((BLOCK, KV_DUP_W), BF16), pltpu.VMEM((BLOCK, KV_DUP_W), BF16),
                pltpu.VMEM((GLA_HEADS, GLA_DV, GLA_DK), F32)]
    return pl.pallas_call(
        _mixer_kernel,
        grid=(B, nt + 1),
        in_specs=[
            pl.BlockSpec(memory_space=pltpu.SMEM),
            pl.BlockSpec((None, MIX_TM, D), proj_tile),
            pl.BlockSpec((None, N_MOD, D), lambda b, s: (b, 0, 0)),
            pl.BlockSpec((None, 1, D), lambda b, s: (l, 0, 0)),
            pl.BlockSpec((None, D, PROJ_W), lambda b, s: (l, 0, 0), pipeline_mode=single),
            pl.BlockSpec((None, D, LANES), lambda b, s: (l, 0, 0), pipeline_mode=single),
            pl.BlockSpec((None, LANES, GK_W), lambda b, s: (l, 0, 0)),
            pl.BlockSpec((None, 1, GK_W), lambda b, s: (l, 0, 0)),
            pl.BlockSpec(bias_tab.shape, lambda b, s: (0, 0, 0), pipeline_mode=single),
            pl.BlockSpec((None, 1, GLA_DV), lambda b, s: (l, 0, 0)),
        ],
        out_specs=[pl.BlockSpec((None, MIX_TM, ATTN_W), mix_tile),
                   pl.BlockSpec((None, MIX_TM, GV_W), mix_tile)],
        out_shape=[jax.ShapeDtypeStruct((B, S, ATTN_W), BF16), jax.ShapeDtypeStruct((B, S, GV_W), BF16)],
        scratch_shapes=scratch,
        compiler_params=_params("parallel", "arbitrary"),
        name="mixer_core",
    )(sinks, h3, modl, nw, w_in_b, w_gate_p, gate_w2_p, gate_b, bias_tab, gnorm)


def _out_kernel(h_ref, mod_ref, oa_ref, og_ref, w_ref, o_ref):
    ka = oa_ref.shape[1]
    y = jnp.dot(oa_ref[...], w_ref[:ka, :], preferred_element_type=F32)
    y = y + jnp.dot(og_ref[...], w_ref[ka:, :], preferred_element_type=F32)
    o_ref[...] = h_ref[...] + mod_ref[5:6, :] * y


def _out_proj(h, modl, oa, og, w_out, l, seq):
    T, D = h.shape
    tpb = seq // OUT_TM
    return pl.pallas_call(
        _out_kernel,
        grid=(T // OUT_TM,),
        in_specs=[
            pl.BlockSpec((OUT_TM, D), lambda i: (i, 0)),
            pl.BlockSpec((None, N_MOD, D), lambda i: (i // tpb, 0, 0)),
            pl.BlockSpec((OUT_TM, oa.shape[1]), lambda i: (i, 0)),
            pl.BlockSpec((OUT_TM, og.shape[1]), lambda i: (i, 0)),
            pl.BlockSpec((None,) + w_out.shape[1:], lambda i: (l, 0, 0)),
        ],
        out_specs=pl.BlockSpec((OUT_TM, D), lambda i: (i, 0)),
        out_shape=jax.ShapeDtypeStruct((T, D), F32),
        compiler_params=_params("parallel"),
        name="mixer_out_proj",
    )(h, modl, oa, og, w_out)


def kernel(x, c, ada_w, ada_b, norm_ffn1, ffn1_w1, ffn1_w3, ffn1_w2, norm_mix, w_in, gla_gate_w2,
           gla_gate_b, attn_sinks, gla_norm, w_out, norm_ffn2, ffn2_w1, ffn2_w3, ffn2_w2,
           final_ada_w, final_ada_b, final_norm):
    B, S, D = x.shape
    L = ada_w.shape[0]
    T = B * S

    c_pad = jnp.pad(c, ((0, SUBLANES - B), (0, 0)))
    mod = _modulation(c_pad, ada_w, ada_b)[:, :B].reshape(L, B, N_MOD, D)
    fmod = _modulation(c_pad, final_ada_w[None], final_ada_b[None])[0, :B].reshape(B, 2, D)

    w11, w13, w12 = _to_bf16(ffn1_w1), _to_bf16(ffn1_w3), _to_bf16(ffn1_w2)
    w21, w23, w22 = _to_bf16(ffn2_w1), _to_bf16(ffn2_w3), _to_bf16(ffn2_w2)
    assert w_in.shape[-1] == PROJ_W
    w_in_b = _to_bf16(w_in)
    gate0 = _SEG["gate"][0]
    w_gate_p = jnp.pad(w_in[:, :, gate0:], ((0, 0), (0, 0), (0, LANES - GLA_RANK))).astype(BF16)
    w_out_b = _to_bf16(w_out)
    gate_w2_p = jnp.pad(gla_gate_w2, ((0, 0), (0, LANES - GLA_RANK), (0, 0))).astype(BF16)
    gate_b = gla_gate_b.reshape(L, 1, GK_W)
    n1, nm, n2 = (t.reshape(L, 1, D) for t in (norm_ffn1, norm_mix, norm_ffn2))
    gnorm = gla_norm.reshape(L, 1, GLA_DV)
    bias_tab = jnp.asarray(_attn_bias_table())

    fnw = final_norm.reshape(1, D)
    h = x.reshape(T, D)
    for l in range(L):
        modl = mod[l]
        h = _ffn(h, modl, n1, w11, w13, w12, fmod, fnw, l, 0, S)
        oa, og = _mixer_core(h.reshape(B, S, D), modl, nm, w_in_b, w_gate_p, gate_w2_p, gate_b,
                             attn_sinks[l], bias_tab, gnorm, l)
        h = _out_proj(h, modl, oa.reshape(T, ATTN_W), og.reshape(T, GV_W), w_out_b, l, S)
        h = _ffn(h, modl, n2, w21, w23, w22, fmod, fnw, l, 6, S, final=(l == L - 1))
    return h.reshape(B, S, D)
```

```python
import functools

import jax
import jax.numpy as jnp
import numpy as np
from jax import lax
from jax.experimental import pallas as pl
from jax.experimental.pallas import tpu as pltpu

F32 = jnp.float32
BF16 = jnp.bfloat16

ATTN_HD = 64
N_Q_HEADS = 16
N_KV_HEADS = 2
GQA_GROUP = N_Q_HEADS // N_KV_HEADS
PAIRS_PER_KV = GQA_GROUP // 2
WINDOW = 128
BLOCK = 128
GLA_HEADS = 4
GLA_DK = 128
GLA_DV = 256
GLA_RANK = 16
GLA_TAU = 16.0
GLA_CHUNK = 64
N_MOD = 9
EPS = 1e-6
MASK_VALUE = -1e30

LANES = 128
SUBLANES = 8
VMEM_LIMIT_BYTES = 56 * 1024 * 1024

FFN_TM = 1024
FFN_TF = 512
FFN_NORM_START = 3
FFN_NORM_SLICES = 8
FFN_NORM_ROWS = FFN_TM // FFN_NORM_SLICES
MIX_TM = 512
PROJ_PIECE = 256
OUT_TM = 512
MOD_TN = 1024
CAST_ROWS = 512

ATTN_W = N_Q_HEADS * ATTN_HD
KV_DUP_W = N_KV_HEADS * 2 * ATTN_HD
GK_W = GLA_HEADS * GLA_DK
GV_W = GLA_HEADS * GLA_DV
_SEG = {}
_off = 0
for _name, _w in (("qa", ATTN_W), ("kv", 2 * N_KV_HEADS * ATTN_HD), ("qg", GK_W), ("kg", GK_W),
                  ("vg", GV_W), ("rg", GV_W), ("gate", GLA_RANK)):
    _SEG[_name] = (_off, _w)
    _off += _w
PROJ_W = _off


def _params(*sem):
    return pltpu.CompilerParams(dimension_semantics=sem, vmem_limit_bytes=VMEM_LIMIT_BYTES)


def _rms_mod(x, nw, scale, shift):
    y = x * lax.rsqrt(jnp.mean(x * x, axis=-1, keepdims=True) + EPS) * nw
    return y * (1.0 + scale) + shift


def _silu(x):
    return x * jax.nn.sigmoid(x)


def _or_zero_of(x, dep):
    bits = lax.bitcast_convert_type(dep.astype(F32), jnp.uint32)
    bits = bits.reshape(dep.shape[0] // SUBLANES, SUBLANES, dep.shape[1])
    fold = functools.reduce(jnp.bitwise_or, [bits[k] for k in range(bits.shape[0])])
    fold = functools.reduce(jnp.bitwise_or,
                            [fold[:, k * LANES:(k + 1) * LANES] for k in range(fold.shape[1] // LANES)])
    zero = lax.shift_right_logical(lax.shift_right_logical(fold, jnp.uint32(16)), jnp.uint32(16))
    zero = jnp.concatenate([zero] * (x.shape[1] // LANES), axis=1)
    xb = lax.bitcast_convert_type(x, jnp.uint32).reshape(x.shape[0] // SUBLANES, SUBLANES, x.shape[1])
    return lax.bitcast_convert_type((xb | zero[None]).reshape(x.shape), F32)


def _cast_kernel(x_ref, o_ref):
    o_ref[...] = x_ref[...].astype(BF16)


def _to_bf16(w):
    L, R, C = w.shape
    return pl.pallas_call(
        _cast_kernel,
        grid=(L, R // CAST_ROWS),
        in_specs=[pl.BlockSpec((None, CAST_ROWS, C), lambda l, r: (l, r, 0))],
        out_specs=pl.BlockSpec((None, CAST_ROWS, C), lambda l, r: (l, r, 0)),
        out_shape=jax.ShapeDtypeStruct(w.shape, BF16),
        compiler_params=_params("parallel", "parallel"),
        name="weight_to_bf16",
    )(w)


def _mod_kernel(c_ref, w_ref, b_ref, o_ref):
    ca = _silu(c_ref[...]).astype(BF16)
    o_ref[...] = jnp.dot(ca, w_ref[...].astype(BF16), preferred_element_type=F32) + b_ref[...]


def _modulation(c_pad, w, b):
    L, D, N = w.shape
    rows = c_pad.shape[0]
    return pl.pallas_call(
        _mod_kernel,
        grid=(L, N // MOD_TN),
        in_specs=[
            pl.BlockSpec((rows, D), lambda l, j: (0, 0)),
            pl.BlockSpec((None, D, MOD_TN), lambda l, j: (l, 0, j)),
            pl.BlockSpec((None, 1, MOD_TN), lambda l, j: (l, 0, j)),
        ],
        out_specs=pl.BlockSpec((None, rows, MOD_TN), lambda l, j: (l, 0, j)),
        out_shape=jax.ShapeDtypeStruct((L, rows, N), F32),
        compiler_params=_params("parallel", "parallel"),
        name="adaln_modulation",
    )(c_pad, w, b.reshape(L, 1, N))


def _ffn_kernel(h_hbm, modc_ref, modn_ref, nw_ref, w1_ref, w3_ref, w2_ref, fmod_ref, fnw_ref,
                o_hbm, hbuf, acc_ref, u0_ref, u1_ref, sem_in, sem_out, *, row0, nf, nt, final):
    i = pl.program_id(0)
    f = pl.program_id(1)
    p = i & 1
    q = 1 - p

    def tile_rows(t):
        return pl.ds(pl.multiple_of(t * FFN_TM, FFN_TM), FFN_TM)

    def read(t, slot):
        return pltpu.make_async_copy(h_hbm.at[tile_rows(t), :], hbuf.at[slot], sem_in.at[slot])

    def write(t, slot):
        return pltpu.make_async_copy(hbuf.at[slot], o_hbm.at[tile_rows(t), :], sem_out.at[slot])

    def normed(x, mod_ref):
        u = _rms_mod(x, nw_ref[...], mod_ref[row0 + 1:row0 + 2, :], mod_ref[row0:row0 + 1, :])
        return u.astype(BF16)

    @pl.when(jnp.logical_and(i == 0, f == 0))
    def _():
        read(0, 0).start()
        read(0, 0).wait()
        u0_ref[...] = normed(hbuf[0], modc_ref)

    @pl.when(f == 1)
    def _():
        @pl.when(i >= 1)
        def _():
            write(i - 1, q).wait()

        @pl.when(i + 1 < nt)
        def _():
            read(i + 1, q).start()

    @pl.when(jnp.logical_and(f == FFN_NORM_START, i + 1 < nt))
    def _():
        read(i + 1, q).wait()

    def step(slot, u_cur, u_nxt, first, last):
        src = jnp.where(f < FFN_NORM_START, p, q)
        sl = jnp.clip(f - FFN_NORM_START, 0, FFN_NORM_SLICES - 1)
        rows = pl.ds(pl.multiple_of(sl * FFN_NORM_ROWS, FFN_NORM_ROWS), FFN_NORM_ROWS)
        u_nxt[rows, :] = normed(hbuf[src, rows, :], modn_ref)
        u = u_cur[...]
        a = jnp.dot(u, w1_ref[...], preferred_element_type=F32)
        b = jnp.dot(u, w3_ref[...], preferred_element_type=F32)
        a = _or_zero_of(a, u_nxt[rows, :])
        hm = (_silu(a) * b).astype(BF16)
        total = jnp.dot(hm, w2_ref[...], preferred_element_type=F32)
        if not first:
            total = acc_ref[...] + total
        if not last:
            acc_ref[...] = total
            return
        y = hbuf[slot] + (0.5 * modc_ref[row0 + 2:row0 + 3, :]) * total
        if final:
            y = _rms_mod(y, fnw_ref[...], fmod_ref[1:2, :], fmod_ref[0:1, :])
        hbuf[slot] = y

    for slot, (u_cur, u_nxt) in enumerate(((u0_ref, u1_ref), (u1_ref, u0_ref))):
        in_slot = p == slot
        pl.when(jnp.logical_and(in_slot, f == 0))(
            functools.partial(step, slot, u_cur, u_nxt, True, False))
        pl.when(jnp.logical_and(in_slot, jnp.logical_and(f > 0, f < nf - 1)))(
            functools.partial(step, slot, u_cur, u_nxt, False, False))
        pl.when(jnp.logical_and(in_slot, f == nf - 1))(
            functools.partial(step, slot, u_cur, u_nxt, False, True))

    @pl.when(f == nf - 1)
    def _():
        write(i, p).start()

        @pl.when(i == nt - 1)
        def _():
            write(i, p).wait()


def _ffn(h, modl, nw, w1, w3, w2, fmod, fnw, l, row0, seq, final=False):
    T, D = h.shape
    F = w1.shape[2]
    nf = F // FFN_TF
    nt = T // FFN_TM
    tpb = seq // FFN_TM
    assert nf == FFN_NORM_START + FFN_NORM_SLICES
    nxt = lambda i: jnp.minimum(i + 1, nt - 1)
    return pl.pallas_call(
        functools.partial(_ffn_kernel, row0=row0, nf=nf, nt=nt, final=final),
        grid=(nt, nf),
        in_specs=[
            pl.BlockSpec(memory_space=pl.ANY),
            pl.BlockSpec((None, N_MOD, D), lambda i, f: (i // tpb, 0, 0)),
            pl.BlockSpec((None, N_MOD, D), lambda i, f: (nxt(i) // tpb, 0, 0)),
            pl.BlockSpec((None, 1, D), lambda i, f: (l, 0, 0)),
            pl.BlockSpec((None, D, FFN_TF), lambda i, f: (l, 0, f)),
            pl.BlockSpec((None, D, FFN_TF), lambda i, f: (l, 0, f)),
            pl.BlockSpec((None, FFN_TF, D), lambda i, f: (l, f, 0)),
            pl.BlockSpec((None, 2, D), lambda i, f: (i // tpb, 0, 0)),
            pl.BlockSpec((1, D), lambda i, f: (0, 0)),
        ],
        out_specs=pl.BlockSpec(memory_space=pl.ANY),
        out_shape=jax.ShapeDtypeStruct((T, D), F32),
        scratch_shapes=[
            pltpu.VMEM((2, FFN_TM, D), F32),
            pltpu.VMEM((FFN_TM, D), F32),
            pltpu.VMEM((FFN_TM, D), BF16),
            pltpu.VMEM((FFN_TM, D), BF16),
            pltpu.SemaphoreType.DMA((2,)),
            pltpu.SemaphoreType.DMA((2,)),
        ],
        compiler_params=_params("arbitrary", "arbitrary"),
        name="swiglu_ffn",
    )(h, modl, modl, nw, w1, w3, w2, fmod, fnw)


def _log_sigmoid(x):
    return jnp.minimum(x, 0.0) - jnp.log1p(jnp.exp(-jnp.abs(x)))


_BUF_FIELDS = (("qa", ATTN_W, BF16), ("kd", KV_DUP_W, BF16), ("vd", KV_DUP_W, BF16), ("qg", GK_W, F32),
               ("kg", GK_W, F32), ("vg", GV_W, BF16), ("rg", GV_W, F32), ("gg", GK_W, F32))


def _project_pieces(h_ref, mod_ref, nw_ref, w_ref, wg_ref, gw_ref, gb_ref, buf):
    qa_ref, kd_ref, vd_ref, qg_ref, kg_ref, vg_ref, rg_ref, gg_ref = buf
    u = _rms_mod(h_ref[...], nw_ref[...], mod_ref[4:5, :], mod_ref[3:4, :]).astype(BF16)

    def piece(c0):
        return jnp.dot(u, w_ref[:, c0:c0 + PROJ_PIECE], preferred_element_type=F32)

    def plain(name, dst):
        start, width = _SEG[name]
        for c0 in range(0, width, PROJ_PIECE):
            dst[:, c0:c0 + PROJ_PIECE] = piece(start + c0).astype(dst.dtype)
            yield

    yield from plain("qa", qa_ref)
    kv = piece(_SEG["kv"][0])
    lo = lax.broadcasted_iota(jnp.int32, (kv.shape[0], LANES), 1) < ATTN_HD
    for src, dst in ((kv[:, :LANES], kd_ref), (kv[:, LANES:], vd_ref)):
        swapped = pltpu.roll(src, ATTN_HD, axis=1)
        dst[:, :LANES] = jnp.where(lo, src, swapped).astype(BF16)
        dst[:, LANES:] = jnp.where(lo, swapped, src).astype(BF16)
    yield
    yield from plain("qg", qg_ref)
    yield from plain("kg", kg_ref)
    yield from plain("vg", vg_ref)
    yield from plain("rg", rg_ref)
    gate_lr = jnp.dot(u, wg_ref[...], preferred_element_type=F32).astype(BF16)
    z = jnp.dot(gate_lr, gw_ref[...], preferred_element_type=F32) + gb_ref[...]
    gg_ref[...] = _log_sigmoid(z) / GLA_TAU
    yield


def _attn_bias_table():
    qi = np.arange(BLOCK)[:, None]
    kj = np.arange(2 * BLOCK)[None, :]
    dist = qi + BLOCK - kj
    band = (dist >= 0) & (dist < WINDOW)
    slopes = np.exp2(-8.0 * np.arange(1, N_Q_HEADS + 1, dtype=np.float32) / N_Q_HEADS).astype(np.float32)
    tab = np.empty((N_KV_HEADS, PAIRS_PER_KV * BLOCK, 4 * BLOCK), np.float32)
    for kh in range(N_KV_HEADS):
        for p in range(PAIRS_PER_KV):
            for e in range(2):
                hd = kh * GQA_GROUP + 2 * p + e
                bias = np.where(band, -slopes[hd] * dist.astype(np.float32), np.float32(MASK_VALUE))
                tab[kh, p * BLOCK:(p + 1) * BLOCK, e * 2 * BLOCK:(e + 1) * 2 * BLOCK] = bias
    return tab


def _attention_stage(buf, tk_ref, tv_ref, sink_ref, bias_ref, first, o_ref):
    q_ref, kc_ref, vc_ref = buf[0], buf[1], buf[2]
    tm = q_ref.shape[0]
    lane = lax.broadcasted_iota(jnp.int32, (2 * BLOCK, LANES), 1)
    lo = lane < ATTN_HD
    col = lax.broadcasted_iota(jnp.int32, (1, 4 * BLOCK), 1)
    prev_cols = (col & (2 * BLOCK - 1)) < BLOCK
    first_mask = jnp.where(jnp.logical_and(first, prev_cols), MASK_VALUE, 0.0)
    for j in range(tm // BLOCK):
        rows = slice(j * BLOCK, (j + 1) * BLOCK)
        for kh in range(N_KV_HEADS):
            cols = slice(kh * LANES, (kh + 1) * LANES)
            if j == 0:
                kprev, vprev = tk_ref[:, cols], tv_ref[:, cols]
            else:
                prows = slice((j - 1) * BLOCK, j * BLOCK)
                kprev, vprev = kc_ref[prows, cols], vc_ref[prows, cols]
            kk = jnp.concatenate([kprev, kc_ref[rows, cols]], axis=0)
            vv = jnp.concatenate([vprev, vc_ref[rows, cols]], axis=0)
            zero = jnp.zeros_like(kk)
            kbd = jnp.concatenate([jnp.where(lo, kk, zero), jnp.where(lo, zero, kk)], axis=0)
            vbd = jnp.concatenate([jnp.where(lo, vv, zero), jnp.where(lo, zero, vv)], axis=0)
            qs = jnp.concatenate(
                [q_ref[rows, (kh * PAIRS_PER_KV + p) * LANES:(kh * PAIRS_PER_KV + p + 1) * LANES]
                 for p in range(PAIRS_PER_KV)], axis=0)
            s = lax.dot_general(qs, kbd, (((1,), (1,)), ((), ())), preferred_element_type=F32)
            s = s * (ATTN_HD ** -0.5) + bias_ref[kh]
            if j == 0:
                s = s + first_mask
            prow = []
            for p in range(PAIRS_PER_KV):
                pe = []
                for e in range(2):
                    sink = sink_ref[kh * GQA_GROUP + 2 * p + e]
                    t = s[p * BLOCK:(p + 1) * BLOCK, e * 2 * BLOCK:(e + 1) * 2 * BLOCK]
                    m = jnp.maximum(jnp.max(t, axis=-1, keepdims=True), sink)
                    pexp = jnp.exp(t - m)
                    den = jnp.sum(pexp, axis=-1, keepdims=True) + jnp.exp(sink - m)
                    pe.append((pexp / den).astype(BF16))
                prow.append(jnp.concatenate(pe, axis=1))
            pmat = jnp.concatenate(prow, axis=0)
            o = jnp.dot(pmat, vbd, preferred_element_type=F32)
            for p in range(PAIRS_PER_KV):
                c0 = (kh * PAIRS_PER_KV + p) * LANES
                o_ref[rows, c0:c0 + LANES] = o[p * BLOCK:(p + 1) * BLOCK].astype(BF16)
            yield
    tk_ref[...] = kc_ref[tm - BLOCK:tm, :]
    tv_ref[...] = vc_ref[tm - BLOCK:tm, :]


def _gla_stage(buf, nw_ref, st_ref, o_ref):
    q_ref, k_ref, v_ref, r_ref, g_ref = buf[3], buf[4], buf[5], buf[6], buf[7]
    tm = q_ref.shape[0]
    ci = lax.broadcasted_iota(jnp.int32, (GLA_CHUNK, GLA_CHUNK), 0)
    si = lax.broadcasted_iota(jnp.int32, (GLA_CHUNK, GLA_CHUNK), 1)
    causal = si <= ci
    tri = causal.astype(BF16)
    nw = nw_ref[...]
    for c in range(tm // GLA_CHUNK):
        rows = slice(c * GLA_CHUNK, (c + 1) * GLA_CHUNK)
        g = g_ref[rows, :]
        g_hi = g.astype(BF16)
        g_r1 = g - g_hi.astype(F32)
        g_mid = g_r1.astype(BF16)
        g_lo = (g_r1 - g_mid.astype(F32)).astype(BF16)
        b = (jnp.dot(tri, g_hi, preferred_element_type=F32)
             + jnp.dot(tri, g_mid, preferred_element_type=F32)
             + jnp.dot(tri, g_lo, preferred_element_type=F32))
        b_last = b[GLA_CHUNK - 1:GLA_CHUNK, :]
        q = q_ref[rows, :] * (GLA_DK ** -0.5)
        k = k_ref[rows, :]
        q_t = (q * jnp.exp(b)).astype(BF16)
        k_t = (k * jnp.exp(-b)).astype(BF16)
        k_end = (k * jnp.exp(b_last - b)).astype(BF16)
        decay = jnp.exp(b_last)
        for hd in range(GLA_HEADS):
            kc = slice(hd * GLA_DK, (hd + 1) * GLA_DK)
            vc = slice(hd * GLA_DV, (hd + 1) * GLA_DV)
            v = v_ref[rows, vc]
            a = lax.dot_general(q_t[:, kc], k_t[:, kc], (((1,), (1,)), ((), ())),
                                preferred_element_type=F32)
            a = jnp.where(causal, a, 0.0).astype(BF16)
            st = st_ref[hd]
            o = jnp.dot(a, v, preferred_element_type=F32)
            o = o + lax.dot_general(q_t[:, kc], st.astype(BF16), (((1,), (1,)), ((), ())),
                                    preferred_element_type=F32)
            upd = lax.dot_general(v, k_end[:, kc], (((0,), (0,)), ((), ())),
                                  preferred_element_type=F32)
            st_ref[hd] = st * decay[:, kc] + upd
            o = o * lax.rsqrt(jnp.mean(o * o, axis=-1, keepdims=True) + EPS) * nw
            o_ref[rows, vc] = (o * _silu(r_ref[rows, vc])).astype(BF16)
        yield


def _mixer_kernel(sink_ref, h_ref, mod_ref, nw_ref, w_ref, wg_ref, gw_ref, gb_ref, bias_ref, gnw_ref,
                  oa_ref, og_ref, *scratch):
    nb = len(_BUF_FIELDS)
    bufs = (scratch[:nb], scratch[nb:2 * nb])
    tk_ref, tv_ref, st_ref = scratch[2 * nb:]
    s = pl.program_id(1)

    @pl.when(s == 0)
    def _():
        for r in bufs[1] + (tk_ref, tv_ref, st_ref):
            r[...] = jnp.zeros_like(r)

    def step(project_into, mix_from):
        pieces = _project_pieces(h_ref, mod_ref, nw_ref, w_ref, wg_ref, gw_ref, gb_ref, project_into)
        next(pieces)
        for _ in _attention_stage(mix_from, tk_ref, tv_ref, sink_ref, bias_ref, s == 1, oa_ref):
            next(pieces, None)
        for _ in _gla_stage(mix_from, gnw_ref, st_ref, og_ref):
            next(pieces, None)
        for _ in pieces:
            pass

    @pl.when((s & 1) == 0)
    def _():
        step(bufs[0], bufs[1])

    @pl.when((s & 1) == 1)
    def _():
        step(bufs[1], bufs[0])


def _mixer_core(h3, modl, nw, w_in_b, w_gate_p, gate_w2_p, gate_b, sinks, bias_tab, gnorm, l):
    B, S, D = h3.shape
    nt = S // MIX_TM
    single = pl.Buffered(1)
    proj_tile = lambda b, s: (b, jnp.minimum(s, nt - 1), 0)
    mix_tile = lambda b, s: (b, jnp.maximum(s - 1, 0), 0)
    scratch = [pltpu.VMEM((MIX_TM, w), dt) for _ in range(2) for _, w, dt in _BUF_FIELDS]
    scratch += [pltpu.VMEM((BLOCK, KV_DUP_W), BF16), pltpu.VMEM((BLOCK, KV_DUP_W), BF16),
                pltpu.VMEM((GLA_HEADS, GLA_DV, GLA_DK), F32)]
    return pl.pallas_call(
        _mixer_kernel,
        grid=(B, nt + 1),
        in_specs=[
            pl.BlockSpec(memory_space=pltpu.SMEM),
            pl.BlockSpec((None, MIX_TM, D), proj_tile),
            pl.BlockSpec((None, N_MOD, D), lambda b, s: (b, 0, 0)),
            pl.BlockSpec((None, 1, D), lambda b, s: (l, 0, 0)),
            pl.BlockSpec((None, D, PROJ_W), lambda b, s: (l, 0, 0), pipeline_mode=single),
            pl.BlockSpec((None, D, LANES), lambda b, s: (l, 0, 0), pipeline_mode=single),
            pl.BlockSpec((None, LANES, GK_W), lambda b, s: (l, 0, 0)),
            pl.BlockSpec((None, 1, GK_W), lambda b, s: (l, 0, 0)),
            pl.BlockSpec(bias_tab.shape, lambda b, s: (0, 0, 0), pipeline_mode=single),
            pl.BlockSpec((None, 1, GLA_DV), lambda b, s: (l, 0, 0)),
        ],
        out_specs=[pl.BlockSpec((None, MIX_TM, ATTN_W), mix_tile),
                   pl.BlockSpec((None, MIX_TM, GV_W), mix_tile)],
        out_shape=[jax.ShapeDtypeStruct((B, S, ATTN_W), BF16), jax.ShapeDtypeStruct((B, S, GV_W), BF16)],
        scratch_shapes=scratch,
        compiler_params=_params("parallel", "arbitrary"),
        name="mixer_core",
    )(sinks, h3, modl, nw, w_in_b, w_gate_p, gate_w2_p, gate_b, bias_tab, gnorm)


def _out_kernel(h_ref, mod_ref, oa_ref, og_ref, w_ref, o_ref):
    ka = oa_ref.shape[1]
    y = jnp.dot(oa_ref[...], w_ref[:ka, :], preferred_element_type=F32)
    y = y + jnp.dot(og_ref[...], w_ref[ka:, :], preferred_element_type=F32)
    o_ref[...] = h_ref[...] + mod_ref[5:6, :] * y


def _out_proj(h, modl, oa, og, w_out, l, seq):
    T, D = h.shape
    tpb = seq // OUT_TM
    return pl.pallas_call(
        _out_kernel,
        grid=(T // OUT_TM,),
        in_specs=[
            pl.BlockSpec((OUT_TM, D), lambda i: (i, 0)),
            pl.BlockSpec((None, N_MOD, D), lambda i: (i // tpb, 0, 0)),
            pl.BlockSpec((OUT_TM, oa.shape[1]), lambda i: (i, 0)),
            pl.BlockSpec((OUT_TM, og.shape[1]), lambda i: (i, 0)),
            pl.BlockSpec((None,) + w_out.shape[1:], lambda i: (l, 0, 0)),
        ],
        out_specs=pl.BlockSpec((OUT_TM, D), lambda i: (i, 0)),
        out_shape=jax.ShapeDtypeStruct((T, D), F32),
        compiler_params=_params("parallel"),
        name="mixer_out_proj",
    )(h, modl, oa, og, w_out)


def kernel(x, c, ada_w, ada_b, norm_ffn1, ffn1_w1, ffn1_w3, ffn1_w2, norm_mix, w_in, gla_gate_w2,
           gla_gate_b, attn_sinks, gla_norm, w_out, norm_ffn2, ffn2_w1, ffn2_w3, ffn2_w2,
           final_ada_w, final_ada_b, final_norm):
    B, S, D = x.shape
    L = ada_w.shape[0]
    T = B * S

    c_pad = jnp.pad(c, ((0, SUBLANES - B), (0, 0)))
    mod = _modulation(c_pad, ada_w, ada_b)[:, :B].reshape(L, B, N_MOD, D)
    fmod = _modulation(c_pad, final_ada_w[None], final_ada_b[None])[0, :B].reshape(B, 2, D)

    w11, w13, w12 = _to_bf16(ffn1_w1), _to_bf16(ffn1_w3), _to_bf16(ffn1_w2)
    w21, w23, w22 = _to_bf16(ffn2_w1), _to_bf16(ffn2_w3), _to_bf16(ffn2_w2)
    assert w_in.shape[-1] == PROJ_W
    w_in_b = _to_bf16(w_in)
    gate0 = _SEG["gate"][0]
    w_gate_p = jnp.pad(w_in[:, :, gate0:], ((0, 0), (0, 0), (0, LANES - GLA_RANK))).astype(BF16)
    w_out_b = _to_bf16(w_out)
    gate_w2_p = jnp.pad(gla_gate_w2, ((0, 0), (0, LANES - GLA_RANK), (0, 0))).astype(BF16)
    gate_b = gla_gate_b.reshape(L, 1, GK_W)
    n1, nm, n2 = (t.reshape(L, 1, D) for t in (norm_ffn1, norm_mix, norm_ffn2))
    gnorm = gla_norm.reshape(L, 1, GLA_DV)
    bias_tab = jnp.asarray(_attn_bias_table())

    fnw = final_norm.reshape(1, D)
    h = x.reshape(T, D)
    for l in range(L):
        modl = mod[l]
        h = _ffn(h, modl, n1, w11, w13, w12, fmod, fnw, l, 0, S)
        oa, og = _mixer_core(h.reshape(B, S, D), modl, nm, w_in_b, w_gate_p, gate_w2_p, gate_b,
                             attn_sinks[l], bias_tab, gnorm, l)
        h = _out_proj(h, modl, oa.reshape(T, ATTN_W), og.reshape(T, GV_W), w_out_b, l, S)
        h = _ffn(h, modl, n2, w21, w23, w22, fmod, fnw, l, 6, S, final=(l == L - 1))
    return h.reshape(B, S, D)
```

```python
import functools

import jax
import jax.numpy as jnp
import numpy as np
from jax import lax
from jax.experimental import pallas as pl
from jax.experimental.pallas import tpu as pltpu

F32 = jnp.float32
BF16 = jnp.bfloat16

ATTN_HD = 64
N_Q_HEADS = 16
N_KV_HEADS = 2
GQA_GROUP = N_Q_HEADS // N_KV_HEADS
PAIRS_PER_KV = GQA_GROUP // 2
WINDOW = 128
BLOCK = 128
GLA_HEADS = 4
GLA_DK = 128
GLA_DV = 256
GLA_RANK = 16
GLA_TAU = 16.0
GLA_CHUNK = 64
N_MOD = 9
EPS = 1e-6
MASK_VALUE = -1e30

LANES = 128
SUBLANES = 8
BF16_SUBLANES = 16
VMEM_LIMIT_BYTES = 56 * 1024 * 1024

FFN_TM = 1024
FFN_TF = 512
FFN_NORM_START = 3
FFN_NORM_SLICES = 8
FFN_NORM_ROWS = FFN_TM // FFN_NORM_SLICES
SIDE_SHIFT, SIDE_SCALE, SIDE_GATE, SIDE_NORM_W, SIDE_FINAL_NORM_W, SIDE_FINAL_SHIFT, SIDE_FINAL_SCALE = range(7)
SIDE_ROWS = 8
MIX_TM = 512
PROJ_PIECE = 256
OUT_TM = 512
MOD_TN = 1024
CAST_ROWS = 512

ATTN_W = N_Q_HEADS * ATTN_HD
KV_DUP_W = N_KV_HEADS * 2 * ATTN_HD
GK_W = GLA_HEADS * GLA_DK
GV_W = GLA_HEADS * GLA_DV
_SEG = {}
_off = 0
for _name, _w in (("qa", ATTN_W), ("kv", 2 * N_KV_HEADS * ATTN_HD), ("qg", GK_W), ("kg", GK_W),
                  ("vg", GV_W), ("rg", GV_W), ("gate", GLA_RANK)):
    _SEG[_name] = (_off, _w)
    _off += _w
PROJ_W = _off


def _params(*sem):
    return pltpu.CompilerParams(dimension_semantics=sem, vmem_limit_bytes=VMEM_LIMIT_BYTES)


def _rms_mod(x, nw, scale, shift):
    y = x * lax.rsqrt(jnp.mean(x * x, axis=-1, keepdims=True) + EPS) * nw
    return y * (1.0 + scale) + shift


def _silu(x):
    return x * jax.nn.sigmoid(x)


def _div_nonneg(x, d):
    if d & (d - 1) == 0:
        return lax.shift_right_logical(x, d.bit_length() - 1)
    return lax.div(x, d)


def _or_zero_of(x, dep):
    bits = lax.bitcast_convert_type(dep.astype(F32), jnp.uint32)
    bits = bits.reshape(dep.shape[0] // SUBLANES, SUBLANES, dep.shape[1])
    fold = functools.reduce(jnp.bitwise_or, [bits[k] for k in range(bits.shape[0])])
    fold = functools.reduce(jnp.bitwise_or,
                            [fold[:, k * LANES:(k + 1) * LANES] for k in range(fold.shape[1] // LANES)])
    zero = lax.shift_right_logical(lax.shift_right_logical(fold, jnp.uint32(16)), jnp.uint32(16))
    zero = jnp.concatenate([zero] * (x.shape[1] // LANES), axis=1)
    xb = lax.bitcast_convert_type(x, jnp.uint32).reshape(x.shape[0] // SUBLANES, SUBLANES, x.shape[1])
    return lax.bitcast_convert_type((xb | zero[None]).reshape(x.shape), F32)


def _cast_kernel(x_ref, o_ref):
    o_ref[...] = x_ref[...].astype(BF16)


def _to_bf16(w):
    L, R, C = w.shape
    rows = max(r for r in range(BF16_SUBLANES, CAST_ROWS + 1, BF16_SUBLANES) if R % r == 0)
    return pl.pallas_call(
        _cast_kernel,
        grid=(L, R // rows),
        in_specs=[pl.BlockSpec((None, rows, C), lambda l, r: (l, r, 0))],
        out_specs=pl.BlockSpec((None, rows, C), lambda l, r: (l, r, 0)),
        out_shape=jax.ShapeDtypeStruct(w.shape, BF16),
        compiler_params=_params("parallel", "parallel"),
        name="weight_to_bf16",
    )(w)


def _mod_kernel(c_ref, w_ref, b_ref, o_ref):
    ca = _silu(c_ref[...]).astype(BF16)
    o_ref[...] = jnp.dot(ca, w_ref[...].astype(BF16), preferred_element_type=F32) + b_ref[...]


def _modulation(c_pad, w, b):
    L, D, N = w.shape
    rows = c_pad.shape[0]
    return pl.pallas_call(
        _mod_kernel,
        grid=(L, N // MOD_TN),
        in_specs=[
            pl.BlockSpec((rows, D), lambda l, j: (0, 0)),
            pl.BlockSpec((None, D, MOD_TN), lambda l, j: (l, 0, j)),
            pl.BlockSpec((None, 1, MOD_TN), lambda l, j: (l, 0, j)),
        ],
        out_specs=pl.BlockSpec((None, rows, MOD_TN), lambda l, j: (l, 0, j)),
        out_shape=jax.ShapeDtypeStruct((L, rows, N), F32),
        compiler_params=_params("parallel", "parallel"),
        name="adaln_modulation",
    )(c_pad, w, b.reshape(L, 1, N))


def _ffn_kernel(h_hbm, side_ref, w1_ref, w3_ref, w2_ref,
                o_hbm, hbuf, acc_ref, u0_ref, u1_ref, sem_in, sem_out, *, nf, nt, tpb, final):
    i = pl.program_id(0)
    f = pl.program_id(1)
    p = i & 1
    q = 1 - p
    bc = _div_nonneg(i, tpb)
    bn = _div_nonneg(jnp.minimum(i + 1, nt - 1), tpb)

    def tile_rows(t):
        return pl.ds(pl.multiple_of(t * FFN_TM, FFN_TM), FFN_TM)

    def read(t, slot):
        return pltpu.make_async_copy(h_hbm.at[tile_rows(t), :], hbuf.at[slot], sem_in.at[slot])

    def write(t, slot):
        return pltpu.make_async_copy(hbuf.at[slot], o_hbm.at[tile_rows(t), :], sem_out.at[slot])

    def side(b, row):
        return side_ref[b, row:row + 1, :]

    def normed(x, b):
        u = _rms_mod(x, side(b, SIDE_NORM_W), side(b, SIDE_SCALE), side(b, SIDE_SHIFT))
        return u.astype(BF16)

    @pl.when(jnp.logical_and(i == 0, f == 0))
    def _():
        read(0, 0).start()
        read(0, 0).wait()
        u0_ref[...] = normed(hbuf[0], bc)

    @pl.when(f == 1)
    def _():
        @pl.when(i >= 1)
        def _():
            write(i - 1, q).wait()

        @pl.when(i + 1 < nt)
        def _():
            read(i + 1, q).start()

    @pl.when(jnp.logical_and(f == FFN_NORM_START, i + 1 < nt))
    def _():
        read(i + 1, q).wait()

    def step(slot, u_cur, u_nxt, first, last):
        src = jnp.where(f < FFN_NORM_START, p, q)
        sl = jnp.clip(f - FFN_NORM_START, 0, FFN_NORM_SLICES - 1)
        rows = pl.ds(pl.multiple_of(sl * FFN_NORM_ROWS, FFN_NORM_ROWS), FFN_NORM_ROWS)
        u_nxt[rows, :] = normed(hbuf[src, rows, :], bn)
        u = u_cur[...]
        a = jnp.dot(u, w1_ref[...], preferred_element_type=F32)
        b = jnp.dot(u, w3_ref[...], preferred_element_type=F32)
        a = _or_zero_of(a, u_nxt[rows, :])
        hm = (_silu(a) * b).astype(BF16)
        total = jnp.dot(hm, w2_ref[...], preferred_element_type=F32)
        if not first:
            total = acc_ref[...] + total
        if not last:
            acc_ref[...] = total
            return
        y = hbuf[slot] + (0.5 * side(bc, SIDE_GATE)) * total
        if final:
            y = _rms_mod(y, side(bc, SIDE_FINAL_NORM_W), side(bc, SIDE_FINAL_SCALE), side(bc, SIDE_FINAL_SHIFT))
        hbuf[slot] = y

    for slot, (u_cur, u_nxt) in enumerate(((u0_ref, u1_ref), (u1_ref, u0_ref))):
        in_slot = p == slot
        pl.when(jnp.logical_and(in_slot, f == 0))(
            functools.partial(step, slot, u_cur, u_nxt, True, False))
        pl.when(jnp.logical_and(in_slot, jnp.logical_and(f > 0, f < nf - 1)))(
            functools.partial(step, slot, u_cur, u_nxt, False, False))
        pl.when(jnp.logical_and(in_slot, f == nf - 1))(
            functools.partial(step, slot, u_cur, u_nxt, False, True))

    @pl.when(f == nf - 1)
    def _():
        write(i, p).start()

        @pl.when(i == nt - 1)
        def _():
            write(i, p).wait()


def _ffn_side(modl, row0, nw_l, fnw, fmod):
    B, _, D = modl.shape
    rows = [modl[:, row0:row0 + 3], jnp.broadcast_to(nw_l, (B, 1, D)), jnp.broadcast_to(fnw, (B, 1, D)),
            fmod, jnp.zeros((B, SIDE_ROWS - 7, D), F32)]
    return jnp.concatenate(rows, axis=1)


def _ffn(h, side, w1, w3, w2, l, seq, final=False):
    T, D = h.shape
    F = w1.shape[2]
    nf = F // FFN_TF
    nt = T // FFN_TM
    tpb = seq // FFN_TM
    assert nf == FFN_NORM_START + FFN_NORM_SLICES
    return pl.pallas_call(
        functools.partial(_ffn_kernel, nf=nf, nt=nt, tpb=tpb, final=final),
        grid=(nt, nf),
        in_specs=[
            pl.BlockSpec(memory_space=pl.ANY),
            pl.BlockSpec(side.shape, lambda i, f: (0, 0, 0), pipeline_mode=pl.Buffered(1)),
            pl.BlockSpec((None, D, FFN_TF), lambda i, f: (l, 0, f)),
            pl.BlockSpec((None, D, FFN_TF), lambda i, f: (l, 0, f)),
            pl.BlockSpec((None, FFN_TF, D), lambda i, f: (l, f, 0)),
        ],
        out_specs=pl.BlockSpec(memory_space=pl.ANY),
        out_shape=jax.ShapeDtypeStruct((T, D), F32),
        scratch_shapes=[
            pltpu.VMEM((2, FFN_TM, D), F32),
            pltpu.VMEM((FFN_TM, D), F32),
            pltpu.VMEM((FFN_TM, D), BF16),
            pltpu.VMEM((FFN_TM, D), BF16),
            pltpu.SemaphoreType.DMA((2,)),
            pltpu.SemaphoreType.DMA((2,)),
        ],
        compiler_params=_params("arbitrary", "arbitrary"),
        name="swiglu_ffn",
    )(h, side, w1, w3, w2)


def _log_sigmoid(x):
    return jnp.minimum(x, 0.0) - jnp.log1p(jnp.exp(-jnp.abs(x)))


_BUF_FIELDS = (("qa", ATTN_W, BF16), ("kd", KV_DUP_W, BF16), ("vd", KV_DUP_W, BF16), ("qg", GK_W, F32),
               ("kg", GK_W, F32), ("vg", GV_W, BF16), ("rg", GV_W, F32), ("gg", GK_W, F32))


def _project_pieces(h_ref, mod_ref, nw_ref, w_ref, wg_ref, gw_ref, gb_ref, buf):
    qa_ref, kd_ref, vd_ref, qg_ref, kg_ref, vg_ref, rg_ref, gg_ref = buf
    u = _rms_mod(h_ref[...], nw_ref[...], mod_ref[4:5, :], mod_ref[3:4, :]).astype(BF16)

    def piece(c0):
        return jnp.dot(u, w_ref[:, c0:c0 + PROJ_PIECE], preferred_element_type=F32)

    def plain(name, dst):
        start, width = _SEG[name]
        for c0 in range(0, width, PROJ_PIECE):
            dst[:, c0:c0 + PROJ_PIECE] = piece(start + c0).astype(dst.dtype)
            yield

    yield from plain("qa", qa_ref)
    kv = piece(_SEG["kv"][0])
    lo = lax.broadcasted_iota(jnp.int32, (kv.shape[0], LANES), 1) < ATTN_HD
    for src, dst in ((kv[:, :LANES], kd_ref), (kv[:, LANES:], vd_ref)):
        swapped = pltpu.roll(src, ATTN_HD, axis=1)
        dst[:, :LANES] = jnp.where(lo, src, swapped).astype(BF16)
        dst[:, LANES:] = jnp.where(lo, swapped, src).astype(BF16)
    yield
    yield from plain("qg", qg_ref)
    yield from plain("kg", kg_ref)
    yield from plain("vg", vg_ref)
    yield from plain("rg", rg_ref)
    gate_lr = jnp.dot(u, wg_ref[...], preferred_element_type=F32).astype(BF16)
    z = jnp.dot(gate_lr, gw_ref[...], preferred_element_type=F32) + gb_ref[...]
    gg_ref[...] = _log_sigmoid(z) / GLA_TAU
    yield


def _attn_bias_table():
    qi = np.arange(BLOCK)[:, None]
    kj = np.arange(2 * BLOCK)[None, :]
    dist = qi + BLOCK - kj
    band = (dist >= 0) & (dist < WINDOW)
    slopes = np.exp2(-8.0 * np.arange(1, N_Q_HEADS + 1, dtype=np.float32) / N_Q_HEADS).astype(np.float32)
    tab = np.empty((N_KV_HEADS, PAIRS_PER_KV * BLOCK, 4 * BLOCK), np.float32)
    for kh in range(N_KV_HEADS):
        for p in range(PAIRS_PER_KV):
            for e in range(2):
                hd = kh * GQA_GROUP + 2 * p + e
                bias = np.where(band, -slopes[hd] * dist.astype(np.float32), np.float32(MASK_VALUE))
                tab[kh, p * BLOCK:(p + 1) * BLOCK, e * 2 * BLOCK:(e + 1) * 2 * BLOCK] = bias
    return tab


def _attention_stage(buf, tk_ref, tv_ref, sink_ref, bias_ref, first, o_ref):
    q_ref, kc_ref, vc_ref = buf[0], buf[1], buf[2]
    tm = q_ref.shape[0]
    lane = lax.broadcasted_iota(jnp.int32, (2 * BLOCK, LANES), 1)
    lo = lane < ATTN_HD
    col = lax.broadcasted_iota(jnp.int32, (1, 4 * BLOCK), 1)
    prev_cols = (col & (2 * BLOCK - 1)) < BLOCK
    first_mask = jnp.where(jnp.logical_and(first, prev_cols), MASK_VALUE, 0.0)
    for j in range(tm // BLOCK):
        rows = slice(j * BLOCK, (j + 1) * BLOCK)
        for kh in range(N_KV_HEADS):
            cols = slice(kh * LANES, (kh + 1) * LANES)
            if j == 0:
                kprev, vprev = tk_ref[:, cols], tv_ref[:, cols]
            else:
                prows = slice((j - 1) * BLOCK, j * BLOCK)
                kprev, vprev = kc_ref[prows, cols], vc_ref[prows, cols]
            kk = jnp.concatenate([kprev, kc_ref[rows, cols]], axis=0)
            vv = jnp.concatenate([vprev, vc_ref[rows, cols]], axis=0)
            zero = jnp.zeros_like(kk)
            kbd = jnp.concatenate([jnp.where(lo, kk, zero), jnp.where(lo, zero, kk)], axis=0)
            vbd = jnp.concatenate([jnp.where(lo, vv, zero), jnp.where(lo, zero, vv)], axis=0)
            qs = jnp.concatenate(
                [q_ref[rows, (kh * PAIRS_PER_KV + p) * LANES:(kh * PAIRS_PER_KV + p + 1) * LANES]
                 for p in range(PAIRS_PER_KV)], axis=0)
            s = lax.dot_general(qs, kbd, (((1,), (1,)), ((), ())), preferred_element_type=F32)
            s = s * (ATTN_HD ** -0.5) + bias_ref[kh]
            if j == 0:
                s = s + first_mask
            prow = []
            for p in range(PAIRS_PER_KV):
                pe = []
                for e in range(2):
                    sink = sink_ref[kh * GQA_GROUP + 2 * p + e]
                    t = s[p * BLOCK:(p + 1) * BLOCK, e * 2 * BLOCK:(e + 1) * 2 * BLOCK]
                    m = jnp.maximum(jnp.max(t, axis=-1, keepdims=True), sink)
                    pexp = jnp.exp(t - m)
                    den = jnp.sum(pexp, axis=-1, keepdims=True) + jnp.exp(sink - m)
                    pe.append((pexp / den).astype(BF16))
                prow.append(jnp.concatenate(pe, axis=1))
            pmat = jnp.concatenate(prow, axis=0)
            o = jnp.dot(pmat, vbd, preferred_element_type=F32)
            for p in range(PAIRS_PER_KV):
                c0 = (kh * PAIRS_PER_KV + p) * LANES
                o_ref[rows, c0:c0 + LANES] = o[p * BLOCK:(p + 1) * BLOCK].astype(BF16)
            yield
    tk_ref[...] = kc_ref[tm - BLOCK:tm, :]
    tv_ref[...] = vc_ref[tm - BLOCK:tm, :]


def _gla_stage(buf, nw_ref, st_ref, o_ref):
    q_ref, k_ref, v_ref, r_ref, g_ref = buf[3], buf[4], buf[5], buf[6], buf[7]
    tm = q_ref.shape[0]
    ci = lax.broadcasted_iota(jnp.int32, (GLA_CHUNK, GLA_CHUNK), 0)
    si = lax.broadcasted_iota(jnp.int32, (GLA_CHUNK, GLA_CHUNK), 1)
    causal = si <= ci
    tri = causal.astype(BF16)
    nw = nw_ref[...]
    for c in range(tm // GLA_CHUNK):
        rows = slice(c * GLA_CHUNK, (c + 1) * GLA_CHUNK)
        g = g_ref[rows, :]
        g_hi = g.astype(BF16)
        g_r1 = g - g_hi.astype(F32)
        g_mid = g_r1.astype(BF16)
        g_lo = (g_r1 - g_mid.astype(F32)).astype(BF16)
        b = (jnp.dot(tri, g_hi, preferred_element_type=F32)
             + jnp.dot(tri, g_mid, preferred_element_type=F32)
             + jnp.dot(tri, g_lo, preferred_element_type=F32))
        b_last = b[GLA_CHUNK - 1:GLA_CHUNK, :]
        q = q_ref[rows, :] * (GLA_DK ** -0.5)
        k = k_ref[rows, :]
        q_t = (q * jnp.exp(b)).astype(BF16)
        k_t = (k * jnp.exp(-b)).astype(BF16)
        k_end = (k * jnp.exp(b_last - b)).astype(BF16)
        decay = jnp.exp(b_last)
        for hd in range(GLA_HEADS):
            kc = slice(hd * GLA_DK, (hd + 1) * GLA_DK)
            vc = slice(hd * GLA_DV, (hd + 1) * GLA_DV)
            v = v_ref[rows, vc]
            a = lax.dot_general(q_t[:, kc], k_t[:, kc], (((1,), (1,)), ((), ())),
                                preferred_element_type=F32)
            a = jnp.where(causal, a, 0.0).astype(BF16)
            st = st_ref[hd]
            o = jnp.dot(a, v, preferred_element_type=F32)
            o = o + lax.dot_general(q_t[:, kc], st.astype(BF16), (((1,), (1,)), ((), ())),
                                    preferred_element_type=F32)
            upd = lax.dot_general(v, k_end[:, kc], (((0,), (0,)), ((), ())),
                                  preferred_element_type=F32)
            st_ref[hd] = st * decay[:, kc] + upd
            o = o * lax.rsqrt(jnp.mean(o * o, axis=-1, keepdims=True) + EPS) * nw
            o_ref[rows, vc] = (o * _silu(r_ref[rows, vc])).astype(BF16)
        yield


def _mixer_kernel(sink_ref, h_ref, mod_ref, nw_ref, w_ref, wg_ref, gw_ref, gb_ref, bias_ref, gnw_ref,
                  oa_ref, og_ref, *scratch):
    nb = len(_BUF_FIELDS)
    bufs = (scratch[:nb], scratch[nb:2 * nb])
    tk_ref, tv_ref, st_ref = scratch[2 * nb:]
    s = pl.program_id(1)

    @pl.when(s == 0)
    def _():
        for r in bufs[1] + (tk_ref, tv_ref, st_ref):
            r[...] = jnp.zeros_like(r)

    def step(project_into, mix_from):
        pieces = _project_pieces(h_ref, mod_ref, nw_ref, w_ref, wg_ref, gw_ref, gb_ref, project_into)
        next(pieces)
        for _ in _attention_stage(mix_from, tk_ref, tv_ref, sink_ref, bias_ref, s == 1, oa_ref):
            next(pieces, None)
        for _ in _gla_stage(mix_from, gnw_ref, st_ref, og_ref):
            next(pieces, None)
        for _ in pieces:
            pass

    @pl.when((s & 1) == 0)
    def _():
        step(bufs[0], bufs[1])

    @pl.when((s & 1) == 1)
    def _():
        step(bufs[1], bufs[0])


def _mixer_core(h3, modl, nw, w_in_b, w_gate_p, gate_w2_p, gate_b, sinks, bias_tab, gnorm, l):
    B, S, D = h3.shape
    nt = S // MIX_TM
    single = pl.Buffered(1)
    proj_tile = lambda b, s: (b, jnp.minimum(s, nt - 1), 0)
    mix_tile = lambda b, s: (b, jnp.maximum(s - 1, 0), 0)
    scratch = [pltpu.VMEM((MIX_TM, w), dt) for _ in range(2) for _, w, dt in _BUF_FIELDS]
    scratch += [pltpu.VMEM((BLOCK, KV_DUP_W), BF16), pltpu.VMEM((BLOCK, KV_DUP_W), BF16),
                pltpu.VMEM((GLA_HEADS, GLA_DV, GLA_DK), F32)]
    return pl.pallas_call(
        _mixer_kernel,
        grid=(B, nt + 1),
        in_specs=[
            pl.BlockSpec(memory_space=pltpu.SMEM),
            pl.BlockSpec((None, MIX_TM, D), proj_tile),
            pl.BlockSpec((None, N_MOD, D), lambda b, s: (b, 0, 0)),
            pl.BlockSpec((None, 1, D), lambda b, s: (l, 0, 0)),
            pl.BlockSpec((None, D, PROJ_W), lambda b, s: (l, 0, 0), pipeline_mode=single),
            pl.BlockSpec((None, D, LANES), lambda b, s: (l, 0, 0), pipeline_mode=single),
            pl.BlockSpec((None, LANES, GK_W), lambda b, s: (l, 0, 0)),
            pl.BlockSpec((None, 1, GK_W), lambda b, s: (l, 0, 0)),
            pl.BlockSpec(bias_tab.shape, lambda b, s: (0, 0, 0), pipeline_mode=single),
            pl.BlockSpec((None, 1, GLA_DV), lambda b, s: (l, 0, 0)),
        ],
        out_specs=[pl.BlockSpec((None, MIX_TM, ATTN_W), mix_tile),
                   pl.BlockSpec((None, MIX_TM, GV_W), mix_tile)],
        out_shape=[jax.ShapeDtypeStruct((B, S, ATTN_W), BF16), jax.ShapeDtypeStruct((B, S, GV_W), BF16)],
        scratch_shapes=scratch,
        compiler_params=_params("parallel", "arbitrary"),
        name="mixer_core",
    )(sinks, h3, modl, nw, w_in_b, w_gate_p, gate_w2_p, gate_b, bias_tab, gnorm)


def _out_kernel(h_ref, mod_ref, oa_ref, og_ref, w_ref, o_ref):
    ka = oa_ref.shape[1]
    y = jnp.dot(oa_ref[...], w_ref[:ka, :], preferred_element_type=F32)
    y = y + jnp.dot(og_ref[...], w_ref[ka:, :], preferred_element_type=F32)
    o_ref[...] = h_ref[...] + mod_ref[5:6, :] * y


def _out_proj(h, modl, oa, og, w_out, l, seq):
    T, D = h.shape
    tpb = seq // OUT_TM
    return pl.pallas_call(
        _out_kernel,
        grid=(T // OUT_TM,),
        in_specs=[
            pl.BlockSpec((OUT_TM, D), lambda i: (i, 0)),
            pl.BlockSpec((None, N_MOD, D), lambda i: (i // tpb, 0, 0)),
            pl.BlockSpec((OUT_TM, oa.shape[1]), lambda i: (i, 0)),
            pl.BlockSpec((OUT_TM, og.shape[1]), lambda i: (i, 0)),
            pl.BlockSpec((None,) + w_out.shape[1:], lambda i: (l, 0, 0)),
        ],
        out_specs=pl.BlockSpec((OUT_TM, D), lambda i: (i, 0)),
        out_shape=jax.ShapeDtypeStruct((T, D), F32),
        compiler_params=_params("parallel"),
        name="mixer_out_proj",
    )(h, modl, oa, og, w_out)


def kernel(x, c, ada_w, ada_b, norm_ffn1, ffn1_w1, ffn1_w3, ffn1_w2, norm_mix, w_in, gla_gate_w2,
           gla_gate_b, attn_sinks, gla_norm, w_out, norm_ffn2, ffn2_w1, ffn2_w3, ffn2_w2,
           final_ada_w, final_ada_b, final_norm):
    B, S, D = x.shape
    L = ada_w.shape[0]
    T = B * S

    c_pad = jnp.pad(c, ((0, SUBLANES - B), (0, 0)))
    mod = _modulation(c_pad, ada_w, ada_b)[:, :B].reshape(L, B, N_MOD, D)
    fmod = _modulation(c_pad, final_ada_w[None], final_ada_b[None])[0, :B].reshape(B, 2, D)

    w11, w13, w12 = _to_bf16(ffn1_w1), _to_bf16(ffn1_w3), _to_bf16(ffn1_w2)
    w21, w23, w22 = _to_bf16(ffn2_w1), _to_bf16(ffn2_w3), _to_bf16(ffn2_w2)
    assert w_in.shape[-1] == PROJ_W
    w_in_b = _to_bf16(w_in)
    gate0 = _SEG["gate"][0]
    w_gate_p = jnp.pad(w_in[:, :, gate0:], ((0, 0), (0, 0), (0, LANES - GLA_RANK))).astype(BF16)
    w_out_b = _to_bf16(w_out)
    gate_w2_p = jnp.pad(gla_gate_w2, ((0, 0), (0, LANES - GLA_RANK), (0, 0))).astype(BF16)
    gate_b = gla_gate_b.reshape(L, 1, GK_W)
    n1, nm, n2 = (t.reshape(L, 1, D) for t in (norm_ffn1, norm_mix, norm_ffn2))
    gnorm = gla_norm.reshape(L, 1, GLA_DV)
    bias_tab = jnp.asarray(_attn_bias_table())

    fnw = final_norm.reshape(1, D)
    h = x.reshape(T, D)
    for l in range(L):
        modl = mod[l]
        h = _ffn(h, _ffn_side(modl, 0, n1[l], fnw, fmod), w11, w13, w12, l, S)
        oa, og = _mixer_core(h.reshape(B, S, D), modl, nm, w_in_b, w_gate_p, gate_w2_p, gate_b,
                             attn_sinks[l], bias_tab, gnorm, l)
        h = _out_proj(h, modl, oa.reshape(T, ATTN_W), og.reshape(T, GV_W), w_out_b, l, S)
        h = _ffn(h, _ffn_side(modl, 6, n2[l], fnw, fmod), w21, w23, w22, l, S, final=(l == L - 1))
    return h.reshape(B, S, D)
```

```python
import functools

import jax
import jax.numpy as jnp
import numpy as np
from jax import lax
from jax.experimental import pallas as pl
from jax.experimental.pallas import tpu as pltpu

F32 = jnp.float32
BF16 = jnp.bfloat16

ATTN_HD = 64
N_Q_HEADS = 16
N_KV_HEADS = 2
GQA_GROUP = N_Q_HEADS // N_KV_HEADS
PAIRS_PER_KV = GQA_GROUP // 2
WINDOW = 128
BLOCK = 128
GLA_HEADS = 4
GLA_DK = 128
GLA_DV = 256
GLA_RANK = 16
GLA_TAU = 16.0
GLA_CHUNK = 64
N_MOD = 9
EPS = 1e-6
MASK_VALUE = -1e30

LANES = 128
SUBLANES = 8
BF16_SUBLANES = 16
VMEM_LIMIT_BYTES = 56 * 1024 * 1024

FFN_TM = 1024
FFN_TF = 512
FFN_NORM_START = 3
FFN_NORM_SLICES = 8
FFN_NORM_ROWS = FFN_TM // FFN_NORM_SLICES
SIDE_SHIFT, SIDE_SCALE, SIDE_GATE, SIDE_NORM_W, SIDE_FINAL_NORM_W, SIDE_FINAL_SHIFT, SIDE_FINAL_SCALE = range(7)
SIDE_ROWS = 8
MIX_TM = 512
GLA_CHUNKS_BEFORE_ATTN = 4
GLA_AHEAD = 2
PROJ_PIECE = 256
OUT_TM = 512
MOD_TN = 1024
CAST_ROWS = 512

ATTN_W = N_Q_HEADS * ATTN_HD
KV_DUP_W = N_KV_HEADS * 2 * ATTN_HD
GK_W = GLA_HEADS * GLA_DK
GV_W = GLA_HEADS * GLA_DV
_SEG = {}
_off = 0
for _name, _w in (("qa", ATTN_W), ("kv", 2 * N_KV_HEADS * ATTN_HD), ("qg", GK_W), ("kg", GK_W),
                  ("vg", GV_W), ("rg", GV_W), ("gate", GLA_RANK)):
    _SEG[_name] = (_off, _w)
    _off += _w
PROJ_W = _off


def _params(*sem):
    return pltpu.CompilerParams(dimension_semantics=sem, vmem_limit_bytes=VMEM_LIMIT_BYTES)


def _rms_mod(x, nw, scale, shift):
    y = x * lax.rsqrt(jnp.mean(x * x, axis=-1, keepdims=True) + EPS) * nw
    return y * (1.0 + scale) + shift


def _silu(x):
    return x * jax.nn.sigmoid(x)


def _div_nonneg(x, d):
    if d & (d - 1) == 0:
        return lax.shift_right_logical(x, d.bit_length() - 1)
    return lax.div(x, d)


def _or_zero_of(x, dep):
    bits = lax.bitcast_convert_type(dep.astype(F32), jnp.uint32)
    bits = bits.reshape(dep.shape[0] // SUBLANES, SUBLANES, dep.shape[1])
    fold = functools.reduce(jnp.bitwise_or, [bits[k] for k in range(bits.shape[0])])
    fold = functools.reduce(jnp.bitwise_or,
                            [fold[:, k * LANES:(k + 1) * LANES] for k in range(fold.shape[1] // LANES)])
    zero = lax.shift_right_logical(lax.shift_right_logical(fold, jnp.uint32(16)), jnp.uint32(16))
    zero = jnp.concatenate([zero] * (x.shape[1] // LANES), axis=1)
    xb = lax.bitcast_convert_type(x, jnp.uint32).reshape(x.shape[0] // SUBLANES, SUBLANES, x.shape[1])
    return lax.bitcast_convert_type((xb | zero[None]).reshape(x.shape), F32)


def _cast_kernel(x_ref, o_ref):
    o_ref[...] = x_ref[...].astype(BF16)


def _to_bf16(w):
    L, R, C = w.shape
    rows = max(r for r in range(BF16_SUBLANES, CAST_ROWS + 1, BF16_SUBLANES) if R % r == 0)
    return pl.pallas_call(
        _cast_kernel,
        grid=(L, R // rows),
        in_specs=[pl.BlockSpec((None, rows, C), lambda l, r: (l, r, 0))],
        out_specs=pl.BlockSpec((None, rows, C), lambda l, r: (l, r, 0)),
        out_shape=jax.ShapeDtypeStruct(w.shape, BF16),
        compiler_params=_params("parallel", "parallel"),
        name="weight_to_bf16",
    )(w)


def _mod_kernel(c_ref, w_ref, b_ref, o_ref):
    ca = _silu(c_ref[...]).astype(BF16)
    o_ref[...] = jnp.dot(ca, w_ref[...].astype(BF16), preferred_element_type=F32) + b_ref[...]


def _modulation(c_pad, w, b):
    L, D, N = w.shape
    rows = c_pad.shape[0]
    return pl.pallas_call(
        _mod_kernel,
        grid=(L, N // MOD_TN),
        in_specs=[
            pl.BlockSpec((rows, D), lambda l, j: (0, 0)),
            pl.BlockSpec((None, D, MOD_TN), lambda l, j: (l, 0, j)),
            pl.BlockSpec((None, 1, MOD_TN), lambda l, j: (l, 0, j)),
        ],
        out_specs=pl.BlockSpec((None, rows, MOD_TN), lambda l, j: (l, 0, j)),
        out_shape=jax.ShapeDtypeStruct((L, rows, N), F32),
        compiler_params=_params("parallel", "parallel"),
        name="adaln_modulation",
    )(c_pad, w, b.reshape(L, 1, N))


def _ffn_kernel(h_hbm, side_ref, w1_ref, w3_ref, w2_ref,
                o_hbm, hbuf, acc_ref, u0_ref, u1_ref, sem_in, sem_out, *, nf, nt, tpb, final):
    i = pl.program_id(0)
    f = pl.program_id(1)
    p = i & 1
    q = 1 - p
    bc = _div_nonneg(i, tpb)
    bn = _div_nonneg(jnp.minimum(i + 1, nt - 1), tpb)

    def tile_rows(t):
        return pl.ds(pl.multiple_of(t * FFN_TM, FFN_TM), FFN_TM)

    def read(t, slot):
        return pltpu.make_async_copy(h_hbm.at[tile_rows(t), :], hbuf.at[slot], sem_in.at[slot])

    def write(t, slot):
        return pltpu.make_async_copy(hbuf.at[slot], o_hbm.at[tile_rows(t), :], sem_out.at[slot])

    def side(b, row):
        return side_ref[b, row:row + 1, :]

    def normed(x, b):
        u = _rms_mod(x, side(b, SIDE_NORM_W), side(b, SIDE_SCALE), side(b, SIDE_SHIFT))
        return u.astype(BF16)

    @pl.when(jnp.logical_and(i == 0, f == 0))
    def _():
        read(0, 0).start()
        read(0, 0).wait()
        u0_ref[...] = normed(hbuf[0], bc)

    @pl.when(f == 1)
    def _():
        @pl.when(i >= 1)
        def _():
            write(i - 1, q).wait()

        @pl.when(i + 1 < nt)
        def _():
            read(i + 1, q).start()

    @pl.when(jnp.logical_and(f == FFN_NORM_START, i + 1 < nt))
    def _():
        read(i + 1, q).wait()

    def step(slot, u_cur, u_nxt, first, last):
        src = jnp.where(f < FFN_NORM_START, p, q)
        sl = jnp.clip(f - FFN_NORM_START, 0, FFN_NORM_SLICES - 1)
        rows = pl.ds(pl.multiple_of(sl * FFN_NORM_ROWS, FFN_NORM_ROWS), FFN_NORM_ROWS)
        u_nxt[rows, :] = normed(hbuf[src, rows, :], bn)
        u = u_cur[...]
        a = jnp.dot(u, w1_ref[...], preferred_element_type=F32)
        b = jnp.dot(u, w3_ref[...], preferred_element_type=F32)
        a = _or_zero_of(a, u_nxt[rows, :])
        hm = (_silu(a) * b).astype(BF16)
        total = jnp.dot(hm, w2_ref[...], preferred_element_type=F32)
        if not first:
            total = acc_ref[...] + total
        if not last:
            acc_ref[...] = total
            return
        y = hbuf[slot] + (0.5 * side(bc, SIDE_GATE)) * total
        if final:
            y = _rms_mod(y, side(bc, SIDE_FINAL_NORM_W), side(bc, SIDE_FINAL_SCALE), side(bc, SIDE_FINAL_SHIFT))
        hbuf[slot] = y

    for slot, (u_cur, u_nxt) in enumerate(((u0_ref, u1_ref), (u1_ref, u0_ref))):
        in_slot = p == slot
        pl.when(jnp.logical_and(in_slot, f == 0))(
            functools.partial(step, slot, u_cur, u_nxt, True, False))
        pl.when(jnp.logical_and(in_slot, jnp.logical_and(f > 0, f < nf - 1)))(
            functools.partial(step, slot, u_cur, u_nxt, False, False))
        pl.when(jnp.logical_and(in_slot, f == nf - 1))(
            functools.partial(step, slot, u_cur, u_nxt, False, True))

    @pl.when(f == nf - 1)
    def _():
        write(i, p).start()

        @pl.when(i == nt - 1)
        def _():
            write(i, p).wait()


def _ffn_side(modl, row0, nw_l, fnw, fmod):
    B, _, D = modl.shape
    rows = [modl[:, row0:row0 + 3], jnp.broadcast_to(nw_l, (B, 1, D)), jnp.broadcast_to(fnw, (B, 1, D)),
            fmod, jnp.zeros((B, SIDE_ROWS - 7, D), F32)]
    return jnp.concatenate(rows, axis=1)


def _ffn(h, side, w1, w3, w2, l, seq, final=False):
    T, D = h.shape
    F = w1.shape[2]
    nf = F // FFN_TF
    nt = T // FFN_TM
    tpb = seq // FFN_TM
    assert nf == FFN_NORM_START + FFN_NORM_SLICES
    return pl.pallas_call(
        functools.partial(_ffn_kernel, nf=nf, nt=nt, tpb=tpb, final=final),
        grid=(nt, nf),
        in_specs=[
            pl.BlockSpec(memory_space=pl.ANY),
            pl.BlockSpec(side.shape, lambda i, f: (0, 0, 0), pipeline_mode=pl.Buffered(1)),
            pl.BlockSpec((None, D, FFN_TF), lambda i, f: (l, 0, f)),
            pl.BlockSpec((None, D, FFN_TF), lambda i, f: (l, 0, f)),
            pl.BlockSpec((None, FFN_TF, D), lambda i, f: (l, f, 0)),
        ],
        out_specs=pl.BlockSpec(memory_space=pl.ANY),
        out_shape=jax.ShapeDtypeStruct((T, D), F32),
        scratch_shapes=[
            pltpu.VMEM((2, FFN_TM, D), F32),
            pltpu.VMEM((FFN_TM, D), F32),
            pltpu.VMEM((FFN_TM, D), BF16),
            pltpu.VMEM((FFN_TM, D), BF16),
            pltpu.SemaphoreType.DMA((2,)),
            pltpu.SemaphoreType.DMA((2,)),
        ],
        compiler_params=_params("arbitrary", "arbitrary"),
        name="swiglu_ffn",
    )(h, side, w1, w3, w2)


def _log_sigmoid(x):
    return jnp.minimum(x, 0.0) - jnp.log1p(jnp.exp(-jnp.abs(x)))


_BUF_FIELDS = (("qa", ATTN_W, BF16), ("kd", KV_DUP_W, BF16), ("vd", KV_DUP_W, BF16), ("qg", GK_W, F32),
               ("kg", GK_W, F32), ("vg", GV_W, BF16), ("rg", GV_W, F32), ("gg", GK_W, F32))


def _project_pieces(h_ref, mod_ref, nw_ref, w_ref, wg_ref, gw_ref, gb_ref, buf):
    qa_ref, kd_ref, vd_ref, qg_ref, kg_ref, vg_ref, rg_ref, gg_ref = buf
    u = _rms_mod(h_ref[...], nw_ref[...], mod_ref[4:5, :], mod_ref[3:4, :]).astype(BF16)

    def piece(c0):
        return jnp.dot(u, w_ref[:, c0:c0 + PROJ_PIECE], preferred_element_type=F32)

    def plain(name, dst):
        start, width = _SEG[name]
        for c0 in range(0, width, PROJ_PIECE):
            dst[:, c0:c0 + PROJ_PIECE] = piece(start + c0).astype(dst.dtype)
            yield

    gate_lr = jnp.dot(u, wg_ref[...], preferred_element_type=F32).astype(BF16)
    gg_ref[...] = jnp.dot(gate_lr, gw_ref[...], preferred_element_type=F32) + gb_ref[...]
    yield
    yield from plain("qa", qa_ref)
    kv = piece(_SEG["kv"][0])
    lo = lax.broadcasted_iota(jnp.int32, (kv.shape[0], LANES), 1) < ATTN_HD
    for src, dst in ((kv[:, :LANES], kd_ref), (kv[:, LANES:], vd_ref)):
        swapped = pltpu.roll(src, ATTN_HD, axis=1)
        dst[:, :LANES] = jnp.where(lo, src, swapped).astype(BF16)
        dst[:, LANES:] = jnp.where(lo, swapped, src).astype(BF16)
    yield
    yield from plain("qg", qg_ref)
    yield from plain("kg", kg_ref)
    yield from plain("vg", vg_ref)
    yield from plain("rg", rg_ref)


def _attn_bias_table():
    qi = np.arange(BLOCK)[:, None]
    kj = np.arange(2 * BLOCK)[None, :]
    dist = qi + BLOCK - kj
    band = (dist >= 0) & (dist < WINDOW)
    slopes = np.exp2(-8.0 * np.arange(1, N_Q_HEADS + 1, dtype=np.float32) / N_Q_HEADS).astype(np.float32)
    tab = np.empty((N_KV_HEADS, PAIRS_PER_KV * BLOCK, 4 * BLOCK), np.float32)
    for kh in range(N_KV_HEADS):
        for p in range(PAIRS_PER_KV):
            for e in range(2):
                hd = kh * GQA_GROUP + 2 * p + e
                bias = np.where(band, -slopes[hd] * dist.astype(np.float32), np.float32(MASK_VALUE))
                tab[kh, p * BLOCK:(p + 1) * BLOCK, e * 2 * BLOCK:(e + 1) * 2 * BLOCK] = bias
    return tab


def _attention_stage(buf, tk_ref, tv_ref, sink_ref, bias_ref, first, o_ref):
    q_ref, kc_ref, vc_ref = buf[0], buf[1], buf[2]
    tm = q_ref.shape[0]
    lane = lax.broadcasted_iota(jnp.int32, (2 * BLOCK, LANES), 1)
    lo = lane < ATTN_HD
    col = lax.broadcasted_iota(jnp.int32, (1, 4 * BLOCK), 1)
    prev_cols = (col & (2 * BLOCK - 1)) < BLOCK
    first_mask = jnp.where(jnp.logical_and(first, prev_cols), MASK_VALUE, 0.0)
    def block_diag(prev_ref, cur_ref, j, kh):
        cols = slice(kh * LANES, (kh + 1) * LANES)
        prev = prev_ref[:, cols] if j == 0 else cur_ref[(j - 1) * BLOCK:j * BLOCK, cols]
        x = jnp.concatenate([prev, cur_ref[j * BLOCK:(j + 1) * BLOCK, cols]], axis=0)
        zero = jnp.zeros_like(x)
        return jnp.concatenate([jnp.where(lo, x, zero), jnp.where(lo, zero, x)], axis=0)

    def scores(j, kh):
        rows = slice(j * BLOCK, (j + 1) * BLOCK)
        qs = jnp.concatenate(
            [q_ref[rows, (kh * PAIRS_PER_KV + p) * LANES:(kh * PAIRS_PER_KV + p + 1) * LANES]
             for p in range(PAIRS_PER_KV)], axis=0)
        s = lax.dot_general(qs, block_diag(tk_ref, kc_ref, j, kh), (((1,), (1,)), ((), ())),
                            preferred_element_type=F32)
        s = s * (ATTN_HD ** -0.5) + bias_ref[kh]
        return s + first_mask if j == 0 else s

    order = [(j, kh) for j in range(tm // BLOCK) for kh in range(N_KV_HEADS)]
    ahead = scores(*order[0])
    for n, (j, kh) in enumerate(order):
        rows = slice(j * BLOCK, (j + 1) * BLOCK)
        s = ahead
        if n + 1 < len(order):
            ahead = scores(*order[n + 1])
        vbd = block_diag(tv_ref, vc_ref, j, kh)
        prow = []
        for p in range(PAIRS_PER_KV):
            pe = []
            for e in range(2):
                sink = sink_ref[kh * GQA_GROUP + 2 * p + e]
                t = s[p * BLOCK:(p + 1) * BLOCK, e * 2 * BLOCK:(e + 1) * 2 * BLOCK]
                m = jnp.maximum(jnp.max(t, axis=-1, keepdims=True), sink)
                pexp = jnp.exp(t - m)
                den = jnp.sum(pexp, axis=-1, keepdims=True) + jnp.exp(sink - m)
                pe.append((pexp / den).astype(BF16))
            prow.append(jnp.concatenate(pe, axis=1))
        pmat = jnp.concatenate(prow, axis=0)
        o = jnp.dot(pmat, vbd, preferred_element_type=F32)
        for p in range(PAIRS_PER_KV):
            c0 = (kh * PAIRS_PER_KV + p) * LANES
            o_ref[rows, c0:c0 + LANES] = o[p * BLOCK:(p + 1) * BLOCK].astype(BF16)
        yield
    tk_ref[...] = kc_ref[tm - BLOCK:tm, :]
    tv_ref[...] = vc_ref[tm - BLOCK:tm, :]


def _gla_stage(buf, nw_ref, st_ref, o_ref):
    q_ref, k_ref, v_ref, r_ref, g_ref = buf[3], buf[4], buf[5], buf[6], buf[7]
    tm = q_ref.shape[0]
    ci = lax.broadcasted_iota(jnp.int32, (GLA_CHUNK, GLA_CHUNK), 0)
    si = lax.broadcasted_iota(jnp.int32, (GLA_CHUNK, GLA_CHUNK), 1)
    causal = si <= ci
    tri = causal.astype(BF16)
    nw = nw_ref[...]
    def decayed_operands(c):
        rows = slice(c * GLA_CHUNK, (c + 1) * GLA_CHUNK)
        g = _log_sigmoid(g_ref[rows, :]) / GLA_TAU
        g_hi = g.astype(BF16)
        g_r1 = g - g_hi.astype(F32)
        g_mid = g_r1.astype(BF16)
        g_lo = (g_r1 - g_mid.astype(F32)).astype(BF16)
        b = (jnp.dot(tri, g_hi, preferred_element_type=F32)
             + jnp.dot(tri, g_mid, preferred_element_type=F32)
             + jnp.dot(tri, g_lo, preferred_element_type=F32))
        b_last = b[GLA_CHUNK - 1:GLA_CHUNK, :]
        q = q_ref[rows, :] * (GLA_DK ** -0.5)
        k = k_ref[rows, :]
        q_t = (q * jnp.exp(b)).astype(BF16)
        k_t = (k * jnp.exp(-b)).astype(BF16)
        k_end = (k * jnp.exp(b_last - b)).astype(BF16)
        intra = []
        for hd in range(GLA_HEADS):
            kc = slice(hd * GLA_DK, (hd + 1) * GLA_DK)
            a = lax.dot_general(q_t[:, kc], k_t[:, kc], (((1,), (1,)), ((), ())),
                                preferred_element_type=F32)
            a = jnp.where(causal, a, 0.0).astype(BF16)
            intra.append(jnp.dot(a, v_ref[rows, hd * GLA_DV:(hd + 1) * GLA_DV], preferred_element_type=F32))
        return q_t, intra, k_end, jnp.exp(b_last)

    n_chunks = tm // GLA_CHUNK
    ahead = [decayed_operands(c) for c in range(min(GLA_AHEAD, n_chunks))]
    for c in range(n_chunks):
        rows = slice(c * GLA_CHUNK, (c + 1) * GLA_CHUNK)
        q_t, intra, k_end, decay = ahead.pop(0)
        if c + GLA_AHEAD < n_chunks:
            ahead.append(decayed_operands(c + GLA_AHEAD))
        for hd in range(GLA_HEADS):
            kc = slice(hd * GLA_DK, (hd + 1) * GLA_DK)
            vc = slice(hd * GLA_DV, (hd + 1) * GLA_DV)
            v = v_ref[rows, vc]
            st = st_ref[hd]
            o = intra[hd] + lax.dot_general(q_t[:, kc], st.astype(BF16), (((1,), (1,)), ((), ())),
                                    preferred_element_type=F32)
            upd = lax.dot_general(v, k_end[:, kc], (((0,), (0,)), ((), ())),
                                  preferred_element_type=F32)
            st_ref[hd] = st * decay[:, kc] + upd
            o = o * lax.rsqrt(jnp.mean(o * o, axis=-1, keepdims=True) + EPS) * nw
            o_ref[rows, vc] = (o * _silu(r_ref[rows, vc])).astype(BF16)
        yield


def _mixer_kernel(sink_ref, h_ref, mod_ref, nw_ref, w_ref, wg_ref, gw_ref, gb_ref, bias_ref, gnw_ref,
                  oa_ref, og_ref, *scratch):
    nb = len(_BUF_FIELDS)
    bufs = (scratch[:nb], scratch[nb:2 * nb])
    tk_ref, tv_ref, st_ref = scratch[2 * nb:]
    s = pl.program_id(1)

    @pl.when(s == 0)
    def _():
        for r in bufs[1] + (tk_ref, tv_ref, st_ref):
            r[...] = jnp.zeros_like(r)

    def step(project_into, mix_from):
        pieces = _project_pieces(h_ref, mod_ref, nw_ref, w_ref, wg_ref, gw_ref, gb_ref, project_into)
        next(pieces)
        gla = _gla_stage(mix_from, gnw_ref, st_ref, og_ref)
        for _ in range(GLA_CHUNKS_BEFORE_ATTN):
            next(gla)
            next(pieces, None)
        for _ in _attention_stage(mix_from, tk_ref, tv_ref, sink_ref, bias_ref, s == 1, oa_ref):
            next(pieces, None)
        for _ in gla:
            next(pieces, None)
        for _ in pieces:
            pass

    @pl.when((s & 1) == 0)
    def _():
        step(bufs[0], bufs[1])

    @pl.when((s & 1) == 1)
    def _():
        step(bufs[1], bufs[0])


def _mixer_core(h3, modl, nw, w_in_b, w_gate_p, gate_w2_p, gate_b, sinks, bias_tab, gnorm, l):
    B, S, D = h3.shape
    nt = S // MIX_TM
    single = pl.Buffered(1)
    proj_tile = lambda b, s: (b, jnp.minimum(s, nt - 1), 0)
    mix_tile = lambda b, s: (b, jnp.maximum(s - 1, 0), 0)
    scratch = [pltpu.VMEM((MIX_TM, w), dt) for _ in range(2) for _, w, dt in _BUF_FIELDS]
    scratch += [pltpu.VMEM((BLOCK, KV_DUP_W), BF16), pltpu.VMEM((BLOCK, KV_DUP_W), BF16),
                pltpu.VMEM((GLA_HEADS, GLA_DV, GLA_DK), F32)]
    return pl.pallas_call(
        _mixer_kernel,
        grid=(B, nt + 1),
        in_specs=[
            pl.BlockSpec(memory_space=pltpu.SMEM),
            pl.BlockSpec((None, MIX_TM, D), proj_tile),
            pl.BlockSpec((None, N_MOD, D), lambda b, s: (b, 0, 0)),
            pl.BlockSpec((None, 1, D), lambda b, s: (l, 0, 0)),
            pl.BlockSpec((None, D, PROJ_W), lambda b, s: (l, 0, 0), pipeline_mode=single),
            pl.BlockSpec((None, D, LANES), lambda b, s: (l, 0, 0), pipeline_mode=single),
            pl.BlockSpec((None, LANES, GK_W), lambda b, s: (l, 0, 0)),
            pl.BlockSpec((None, 1, GK_W), lambda b, s: (l, 0, 0)),
            pl.BlockSpec(bias_tab.shape, lambda b, s: (0, 0, 0), pipeline_mode=single),
            pl.BlockSpec((None, 1, GLA_DV), lambda b, s: (l, 0, 0)),
        ],
        out_specs=[pl.BlockSpec((None, MIX_TM, ATTN_W), mix_tile),
                   pl.BlockSpec((None, MIX_TM, GV_W), mix_tile)],
        out_shape=[jax.ShapeDtypeStruct((B, S, ATTN_W), BF16), jax.ShapeDtypeStruct((B, S, GV_W), BF16)],
        scratch_shapes=scratch,
        compiler_params=_params("parallel", "arbitrary"),
        name="mixer_core",
    )(sinks, h3, modl, nw, w_in_b, w_gate_p, gate_w2_p, gate_b, bias_tab, gnorm)


def _out_kernel(h_ref, mod_ref, oa_ref, og_ref, w_ref, o_ref):
    ka = oa_ref.shape[1]
    y = jnp.dot(oa_ref[...], w_ref[:ka, :], preferred_element_type=F32)
    y = y + jnp.dot(og_ref[...], w_ref[ka:, :], preferred_element_type=F32)
    o_ref[...] = h_ref[...] + mod_ref[5:6, :] * y


def _out_proj(h, modl, oa, og, w_out, l, seq):
    T, D = h.shape
    tpb = seq // OUT_TM
    return pl.pallas_call(
        _out_kernel,
        grid=(T // OUT_TM,),
        in_specs=[
            pl.BlockSpec((OUT_TM, D), lambda i: (i, 0)),
            pl.BlockSpec((None, N_MOD, D), lambda i: (i // tpb, 0, 0)),
            pl.BlockSpec((OUT_TM, oa.shape[1]), lambda i: (i, 0)),
            pl.BlockSpec((OUT_TM, og.shape[1]), lambda i: (i, 0)),
            pl.BlockSpec((None,) + w_out.shape[1:], lambda i: (l, 0, 0)),
        ],
        out_specs=pl.BlockSpec((OUT_TM, D), lambda i: (i, 0)),
        out_shape=jax.ShapeDtypeStruct((T, D), F32),
        compiler_params=_params("parallel"),
        name="mixer_out_proj",
    )(h, modl, oa, og, w_out)


def kernel(x, c, ada_w, ada_b, norm_ffn1, ffn1_w1, ffn1_w3, ffn1_w2, norm_mix, w_in, gla_gate_w2,
           gla_gate_b, attn_sinks, gla_norm, w_out, norm_ffn2, ffn2_w1, ffn2_w3, ffn2_w2,
           final_ada_w, final_ada_b, final_norm):
    B, S, D = x.shape
    L = ada_w.shape[0]
    T = B * S

    c_pad = jnp.pad(c, ((0, SUBLANES - B), (0, 0)))
    mod = _modulation(c_pad, ada_w, ada_b)[:, :B].reshape(L, B, N_MOD, D)
    fmod = _modulation(c_pad, final_ada_w[None], final_ada_b[None])[0, :B].reshape(B, 2, D)

    w11, w13, w12 = _to_bf16(ffn1_w1), _to_bf16(ffn1_w3), _to_bf16(ffn1_w2)
    w21, w23, w22 = _to_bf16(ffn2_w1), _to_bf16(ffn2_w3), _to_bf16(ffn2_w2)
    assert w_in.shape[-1] == PROJ_W
    w_in_b = _to_bf16(w_in)
    gate0 = _SEG["gate"][0]
    w_gate_p = jnp.pad(w_in[:, :, gate0:], ((0, 0), (0, 0), (0, LANES - GLA_RANK))).astype(BF16)
    w_out_b = _to_bf16(w_out)
    gate_w2_p = jnp.pad(gla_gate_w2, ((0, 0), (0, LANES - GLA_RANK), (0, 0))).astype(BF16)
    gate_b = gla_gate_b.reshape(L, 1, GK_W)
    n1, nm, n2 = (t.reshape(L, 1, D) for t in (norm_ffn1, norm_mix, norm_ffn2))
    gnorm = gla_norm.reshape(L, 1, GLA_DV)
    bias_tab = jnp.asarray(_attn_bias_table())

    fnw = final_norm.reshape(1, D)
    h = x.reshape(T, D)
    for l in range(L):
        modl = mod[l]
        h = _ffn(h, _ffn_side(modl, 0, n1[l], fnw, fmod), w11, w13, w12, l, S)
        oa, og = _mixer_core(h.reshape(B, S, D), modl, nm, w_in_b, w_gate_p, gate_w2_p, gate_b,
                             attn_sinks[l], bias_tab, gnorm, l)
        h = _out_proj(h, modl, oa.reshape(T, ATTN_W), og.reshape(T, GV_W), w_out_b, l, S)
        h = _ffn(h, _ffn_side(modl, 6, n2[l], fnw, fmod), w21, w23, w22, l, S, final=(l == L - 1))
    return h.reshape(B, S, D)
```

```python
import functools

import jax
import jax.numpy as jnp
import numpy as np
from jax import lax
from jax.experimental import pallas as pl
from jax.experimental.pallas import tpu as pltpu

F32 = jnp.float32
BF16 = jnp.bfloat16

ATTN_HD = 64
N_Q_HEADS = 16
N_KV_HEADS = 2
GQA_GROUP = N_Q_HEADS // N_KV_HEADS
PAIRS_PER_KV = GQA_GROUP // 2
WINDOW = 128
BLOCK = 128
GLA_HEADS = 4
GLA_DK = 128
GLA_DV = 256
GLA_RANK = 16
GLA_TAU = 16.0
GLA_CHUNK = 64
N_MOD = 9
EPS = 1e-6
MASK_VALUE = -1e30

LANES = 128
SUBLANES = 8
BF16_SUBLANES = 16
VMEM_LIMIT_BYTES = 56 * 1024 * 1024

FFN_TM = 1024
FFN_TF = 512
FFN_NORM_START = 3
FFN_NORM_SLICES = 8
FFN_NORM_ROWS = FFN_TM // FFN_NORM_SLICES
SIDE_SHIFT, SIDE_SCALE, SIDE_GATE, SIDE_NORM_W, SIDE_FINAL_NORM_W, SIDE_FINAL_SHIFT, SIDE_FINAL_SCALE = range(7)
SIDE_ROWS = 8
MIX_TM = 512
GLA_CHUNKS_BEFORE_ATTN = 2
GLA_AHEAD = 3
PROJ_PIECE = 256
OUT_TM = 512
MOD_TN = 1024
CAST_ROWS = 512

ATTN_W = N_Q_HEADS * ATTN_HD
KV_DUP_W = N_KV_HEADS * 2 * ATTN_HD
GK_W = GLA_HEADS * GLA_DK
GV_W = GLA_HEADS * GLA_DV
_SEG = {}
_off = 0
for _name, _w in (("qa", ATTN_W), ("kv", 2 * N_KV_HEADS * ATTN_HD), ("qg", GK_W), ("kg", GK_W),
                  ("vg", GV_W), ("rg", GV_W), ("gate", GLA_RANK)):
    _SEG[_name] = (_off, _w)
    _off += _w
PROJ_W = _off


def _params(*sem):
    return pltpu.CompilerParams(dimension_semantics=sem, vmem_limit_bytes=VMEM_LIMIT_BYTES)


def _rms_mod(x, nw, scale, shift):
    y = x * lax.rsqrt(jnp.mean(x * x, axis=-1, keepdims=True) + EPS) * nw
    return y * (1.0 + scale) + shift


def _silu(x):
    return x * jax.nn.sigmoid(x)


def _div_nonneg(x, d):
    if d & (d - 1) == 0:
        return lax.shift_right_logical(x, d.bit_length() - 1)
    return lax.div(x, d)


def _or_zero_of(x, dep):
    bits = lax.bitcast_convert_type(dep.astype(F32), jnp.uint32)
    bits = bits.reshape(dep.shape[0] // SUBLANES, SUBLANES, dep.shape[1])
    fold = functools.reduce(jnp.bitwise_or, [bits[k] for k in range(bits.shape[0])])
    fold = functools.reduce(jnp.bitwise_or,
                            [fold[:, k * LANES:(k + 1) * LANES] for k in range(fold.shape[1] // LANES)])
    zero = lax.shift_right_logical(lax.shift_right_logical(fold, jnp.uint32(16)), jnp.uint32(16))
    zero = jnp.concatenate([zero] * (x.shape[1] // LANES), axis=1)
    xb = lax.bitcast_convert_type(x, jnp.uint32).reshape(x.shape[0] // SUBLANES, SUBLANES, x.shape[1])
    return lax.bitcast_convert_type((xb | zero[None]).reshape(x.shape), F32)


def _cast_kernel(x_ref, o_ref):
    o_ref[...] = x_ref[...].astype(BF16)


def _to_bf16(w):
    L, R, C = w.shape
    rows = max(r for r in range(BF16_SUBLANES, CAST_ROWS + 1, BF16_SUBLANES) if R % r == 0)
    return pl.pallas_call(
        _cast_kernel,
        grid=(L, R // rows),
        in_specs=[pl.BlockSpec((None, rows, C), lambda l, r: (l, r, 0))],
        out_specs=pl.BlockSpec((None, rows, C), lambda l, r: (l, r, 0)),
        out_shape=jax.ShapeDtypeStruct(w.shape, BF16),
        compiler_params=_params("parallel", "parallel"),
        name="weight_to_bf16",
    )(w)


def _mod_kernel(c_ref, w_ref, b_ref, o_ref):
    ca = _silu(c_ref[...]).astype(BF16)
    o_ref[...] = jnp.dot(ca, w_ref[...].astype(BF16), preferred_element_type=F32) + b_ref[...]


def _modulation(c_pad, w, b):
    L, D, N = w.shape
    rows = c_pad.shape[0]
    return pl.pallas_call(
        _mod_kernel,
        grid=(L, N // MOD_TN),
        in_specs=[
            pl.BlockSpec((rows, D), lambda l, j: (0, 0)),
            pl.BlockSpec((None, D, MOD_TN), lambda l, j: (l, 0, j)),
            pl.BlockSpec((None, 1, MOD_TN), lambda l, j: (l, 0, j)),
        ],
        out_specs=pl.BlockSpec((None, rows, MOD_TN), lambda l, j: (l, 0, j)),
        out_shape=jax.ShapeDtypeStruct((L, rows, N), F32),
        compiler_params=_params("parallel", "parallel"),
        name="adaln_modulation",
    )(c_pad, w, b.reshape(L, 1, N))


def _ffn_kernel(h_hbm, side_ref, w1_ref, w3_ref, w2_ref,
                o_hbm, hbuf, acc_ref, u0_ref, u1_ref, sem_in, sem_out, *, nf, nt, tpb, final):
    i = pl.program_id(0)
    f = pl.program_id(1)
    p = i & 1
    q = 1 - p
    bc = _div_nonneg(i, tpb)
    bn = _div_nonneg(jnp.minimum(i + 1, nt - 1), tpb)

    def tile_rows(t):
        return pl.ds(pl.multiple_of(t * FFN_TM, FFN_TM), FFN_TM)

    def read(t, slot):
        return pltpu.make_async_copy(h_hbm.at[tile_rows(t), :], hbuf.at[slot], sem_in.at[slot])

    def write(t, slot):
        return pltpu.make_async_copy(hbuf.at[slot], o_hbm.at[tile_rows(t), :], sem_out.at[slot])

    def side(b, row):
        return side_ref[b, row:row + 1, :]

    def normed(x, b):
        u = _rms_mod(x, side(b, SIDE_NORM_W), side(b, SIDE_SCALE), side(b, SIDE_SHIFT))
        return u.astype(BF16)

    @pl.when(jnp.logical_and(i == 0, f == 0))
    def _():
        read(0, 0).start()
        read(0, 0).wait()
        u0_ref[...] = normed(hbuf[0], bc)

    @pl.when(f == 1)
    def _():
        @pl.when(i >= 1)
        def _():
            write(i - 1, q).wait()

        @pl.when(i + 1 < nt)
        def _():
            read(i + 1, q).start()

    @pl.when(jnp.logical_and(f == FFN_NORM_START, i + 1 < nt))
    def _():
        read(i + 1, q).wait()

    def step(slot, u_cur, u_nxt, first, last):
        src = jnp.where(f < FFN_NORM_START, p, q)
        sl = jnp.clip(f - FFN_NORM_START, 0, FFN_NORM_SLICES - 1)
        rows = pl.ds(pl.multiple_of(sl * FFN_NORM_ROWS, FFN_NORM_ROWS), FFN_NORM_ROWS)
        u_nxt[rows, :] = normed(hbuf[src, rows, :], bn)
        u = u_cur[...]
        a = jnp.dot(u, w1_ref[...], preferred_element_type=F32)
        b = jnp.dot(u, w3_ref[...], preferred_element_type=F32)
        a = _or_zero_of(a, u_nxt[rows, :])
        hm = (_silu(a) * b).astype(BF16)
        total = jnp.dot(hm, w2_ref[...], preferred_element_type=F32)
        if not first:
            total = acc_ref[...] + total
        if not last:
            acc_ref[...] = total
            return
        y = hbuf[slot] + (0.5 * side(bc, SIDE_GATE)) * total
        if final:
            y = _rms_mod(y, side(bc, SIDE_FINAL_NORM_W), side(bc, SIDE_FINAL_SCALE), side(bc, SIDE_FINAL_SHIFT))
        hbuf[slot] = y

    for slot, (u_cur, u_nxt) in enumerate(((u0_ref, u1_ref), (u1_ref, u0_ref))):
        in_slot = p == slot
        pl.when(jnp.logical_and(in_slot, f == 0))(
            functools.partial(step, slot, u_cur, u_nxt, True, False))
        pl.when(jnp.logical_and(in_slot, jnp.logical_and(f > 0, f < nf - 1)))(
            functools.partial(step, slot, u_cur, u_nxt, False, False))
        pl.when(jnp.logical_and(in_slot, f == nf - 1))(
            functools.partial(step, slot, u_cur, u_nxt, False, True))

    @pl.when(f == nf - 1)
    def _():
        write(i, p).start()

        @pl.when(i == nt - 1)
        def _():
            write(i, p).wait()


def _ffn_side(modl, row0, nw_l, fnw, fmod):
    B, _, D = modl.shape
    rows = [modl[:, row0:row0 + 3], jnp.broadcast_to(nw_l, (B, 1, D)), jnp.broadcast_to(fnw, (B, 1, D)),
            fmod, jnp.zeros((B, SIDE_ROWS - 7, D), F32)]
    return jnp.concatenate(rows, axis=1)


def _ffn(h, side, w1, w3, w2, l, seq, final=False):
    T, D = h.shape
    F = w1.shape[2]
    nf = F // FFN_TF
    nt = T // FFN_TM
    tpb = seq // FFN_TM
    assert nf == FFN_NORM_START + FFN_NORM_SLICES
    return pl.pallas_call(
        functools.partial(_ffn_kernel, nf=nf, nt=nt, tpb=tpb, final=final),
        grid=(nt, nf),
        in_specs=[
            pl.BlockSpec(memory_space=pl.ANY),
            pl.BlockSpec(side.shape, lambda i, f: (0, 0, 0), pipeline_mode=pl.Buffered(1)),
            pl.BlockSpec((None, D, FFN_TF), lambda i, f: (l, 0, f)),
            pl.BlockSpec((None, D, FFN_TF), lambda i, f: (l, 0, f)),
            pl.BlockSpec((None, FFN_TF, D), lambda i, f: (l, f, 0)),
        ],
        out_specs=pl.BlockSpec(memory_space=pl.ANY),
        out_shape=jax.ShapeDtypeStruct((T, D), F32),
        scratch_shapes=[
            pltpu.VMEM((2, FFN_TM, D), F32),
            pltpu.VMEM((FFN_TM, D), F32),
            pltpu.VMEM((FFN_TM, D), BF16),
            pltpu.VMEM((FFN_TM, D), BF16),
            pltpu.SemaphoreType.DMA((2,)),
            pltpu.SemaphoreType.DMA((2,)),
        ],
        compiler_params=_params("arbitrary", "arbitrary"),
        name="swiglu_ffn",
    )(h, side, w1, w3, w2)


def _log_sigmoid(x):
    return jnp.minimum(x, 0.0) - jnp.log1p(jnp.exp(-jnp.abs(x)))


_BUF_FIELDS = (("qa", ATTN_W, BF16), ("kd", KV_DUP_W, BF16), ("vd", KV_DUP_W, BF16), ("qg", GK_W, F32),
               ("kg", GK_W, F32), ("vg", GV_W, BF16), ("rg", GV_W, F32), ("gg", GK_W, F32))


def _project_pieces(h_ref, mod_ref, nw_ref, w_ref, wg_ref, gw_ref, gb_ref, buf):
    qa_ref, kd_ref, vd_ref, qg_ref, kg_ref, vg_ref, rg_ref, gg_ref = buf
    u = _rms_mod(h_ref[...], nw_ref[...], mod_ref[4:5, :], mod_ref[3:4, :]).astype(BF16)

    def piece(c0, width=PROJ_PIECE):
        return jnp.dot(u, w_ref[:, c0:c0 + width], preferred_element_type=F32)

    def plain(name, dst):
        start, width = _SEG[name]
        for c0 in range(0, width, PROJ_PIECE):
            dst[:, c0:c0 + PROJ_PIECE] = piece(start + c0).astype(dst.dtype)
            yield

    gate_lr = jnp.dot(u, wg_ref[...], preferred_element_type=F32).astype(BF16)
    gg_ref[...] = jnp.dot(gate_lr, gw_ref[...], preferred_element_type=F32) + gb_ref[...]
    yield
    yield from plain("qa", qa_ref)
    kv = piece(*_SEG["kv"])
    lo = lax.broadcasted_iota(jnp.int32, (kv.shape[0], LANES), 1) < ATTN_HD
    for src, dst in ((kv[:, :LANES], kd_ref), (kv[:, LANES:], vd_ref)):
        swapped = pltpu.roll(src, ATTN_HD, axis=1)
        dst[:, :LANES] = jnp.where(lo, src, swapped).astype(BF16)
        dst[:, LANES:] = jnp.where(lo, swapped, src).astype(BF16)
    yield
    yield from plain("qg", qg_ref)
    yield from plain("kg", kg_ref)
    yield from plain("vg", vg_ref)
    yield from plain("rg", rg_ref)


def _attn_bias_table():
    qi = np.arange(BLOCK)[:, None]
    kj = np.arange(2 * BLOCK)[None, :]
    dist = qi + BLOCK - kj
    band = (dist >= 0) & (dist < WINDOW)
    slopes = np.exp2(-8.0 * np.arange(1, N_Q_HEADS + 1, dtype=np.float32) / N_Q_HEADS).astype(np.float32)
    tab = np.empty((N_KV_HEADS, PAIRS_PER_KV * BLOCK, 4 * BLOCK), np.float32)
    for kh in range(N_KV_HEADS):
        for p in range(PAIRS_PER_KV):
            for e in range(2):
                hd = kh * GQA_GROUP + 2 * p + e
                bias = np.where(band, -slopes[hd] * dist.astype(np.float32), np.float32(MASK_VALUE))
                tab[kh, p * BLOCK:(p + 1) * BLOCK, e * 2 * BLOCK:(e + 1) * 2 * BLOCK] = bias
    return tab


def _attention_stage(buf, tk_ref, tv_ref, sink_ref, bias_ref, first, o_ref):
    q_ref, kc_ref, vc_ref = buf[0], buf[1], buf[2]
    tm = q_ref.shape[0]
    lane = lax.broadcasted_iota(jnp.int32, (2 * BLOCK, LANES), 1)
    lo = lane < ATTN_HD
    col = lax.broadcasted_iota(jnp.int32, (1, 4 * BLOCK), 1)
    prev_cols = (col & (2 * BLOCK - 1)) < BLOCK
    first_mask = jnp.where(jnp.logical_and(first, prev_cols), MASK_VALUE, 0.0)
    def block_diag(prev_ref, cur_ref, j, kh):
        cols = slice(kh * LANES, (kh + 1) * LANES)
        prev = prev_ref[:, cols] if j == 0 else cur_ref[(j - 1) * BLOCK:j * BLOCK, cols]
        x = jnp.concatenate([prev, cur_ref[j * BLOCK:(j + 1) * BLOCK, cols]], axis=0)
        zero = jnp.zeros_like(x)
        return jnp.concatenate([jnp.where(lo, x, zero), jnp.where(lo, zero, x)], axis=0)

    def scores(j, kh):
        rows = slice(j * BLOCK, (j + 1) * BLOCK)
        qs = jnp.concatenate(
            [q_ref[rows, (kh * PAIRS_PER_KV + p) * LANES:(kh * PAIRS_PER_KV + p + 1) * LANES]
             for p in range(PAIRS_PER_KV)], axis=0)
        s = lax.dot_general(qs, block_diag(tk_ref, kc_ref, j, kh), (((1,), (1,)), ((), ())),
                            preferred_element_type=F32)
        s = s * (ATTN_HD ** -0.5) + bias_ref[kh]
        return s + first_mask if j == 0 else s

    order = [(j, kh) for j in range(tm // BLOCK) for kh in range(N_KV_HEADS)]
    ahead = scores(*order[0])
    for n, (j, kh) in enumerate(order):
        rows = slice(j * BLOCK, (j + 1) * BLOCK)
        s = ahead
        if n + 1 < len(order):
            ahead = scores(*order[n + 1])
        vbd = block_diag(tv_ref, vc_ref, j, kh)
        prow = []
        for p in range(PAIRS_PER_KV):
            pe = []
            for e in range(2):
                sink = sink_ref[kh * GQA_GROUP + 2 * p + e]
                t = s[p * BLOCK:(p + 1) * BLOCK, e * 2 * BLOCK:(e + 1) * 2 * BLOCK]
                m = jnp.maximum(jnp.max(t, axis=-1, keepdims=True), sink)
                pexp = jnp.exp(t - m)
                den = jnp.sum(pexp, axis=-1, keepdims=True) + jnp.exp(sink - m)
                pe.append((pexp / den).astype(BF16))
            prow.append(jnp.concatenate(pe, axis=1))
        pmat = jnp.concatenate(prow, axis=0)
        o = jnp.dot(pmat, vbd, preferred_element_type=F32)
        for p in range(PAIRS_PER_KV):
            c0 = (kh * PAIRS_PER_KV + p) * LANES
            o_ref[rows, c0:c0 + LANES] = o[p * BLOCK:(p + 1) * BLOCK].astype(BF16)
        yield
    tk_ref[...] = kc_ref[tm - BLOCK:tm, :]
    tv_ref[...] = vc_ref[tm - BLOCK:tm, :]


def _gla_stage(buf, nw_ref, st_ref, o_ref):
    q_ref, k_ref, v_ref, r_ref, g_ref = buf[3], buf[4], buf[5], buf[6], buf[7]
    tm = q_ref.shape[0]
    ci = lax.broadcasted_iota(jnp.int32, (GLA_CHUNK, GLA_CHUNK), 0)
    si = lax.broadcasted_iota(jnp.int32, (GLA_CHUNK, GLA_CHUNK), 1)
    causal = si <= ci
    tri = causal.astype(BF16)
    nw = nw_ref[...]
    def decayed_operands(c):
        rows = slice(c * GLA_CHUNK, (c + 1) * GLA_CHUNK)
        g = _log_sigmoid(g_ref[rows, :]) / GLA_TAU
        g_hi = g.astype(BF16)
        g_r1 = g - g_hi.astype(F32)
        g_mid = g_r1.astype(BF16)
        g_lo = (g_r1 - g_mid.astype(F32)).astype(BF16)
        b = (jnp.dot(tri, g_hi, preferred_element_type=F32)
             + jnp.dot(tri, g_mid, preferred_element_type=F32)
             + jnp.dot(tri, g_lo, preferred_element_type=F32))
        b_last = b[GLA_CHUNK - 1:GLA_CHUNK, :]
        q = q_ref[rows, :] * (GLA_DK ** -0.5)
        k = k_ref[rows, :]
        q_t = (q * jnp.exp(b)).astype(BF16)
        k_t = (k * jnp.exp(-b)).astype(BF16)
        k_end = (k * jnp.exp(b_last - b)).astype(BF16)
        intra = []
        for hd in range(GLA_HEADS):
            kc = slice(hd * GLA_DK, (hd + 1) * GLA_DK)
            a = lax.dot_general(q_t[:, kc], k_t[:, kc], (((1,), (1,)), ((), ())),
                                preferred_element_type=F32)
            a = jnp.where(causal, a, 0.0).astype(BF16)
            intra.append(jnp.dot(a, v_ref[rows, hd * GLA_DV:(hd + 1) * GLA_DV], preferred_element_type=F32))
        return q_t, intra, k_end, jnp.exp(b_last)

    n_chunks = tm // GLA_CHUNK
    ahead = [decayed_operands(c) for c in range(min(GLA_AHEAD, n_chunks))]
    for c in range(n_chunks):
        rows = slice(c * GLA_CHUNK, (c + 1) * GLA_CHUNK)
        q_t, intra, k_end, decay = ahead.pop(0)
        if c + GLA_AHEAD < n_chunks:
            ahead.append(decayed_operands(c + GLA_AHEAD))
        for hd in range(GLA_HEADS):
            kc = slice(hd * GLA_DK, (hd + 1) * GLA_DK)
            vc = slice(hd * GLA_DV, (hd + 1) * GLA_DV)
            v = v_ref[rows, vc]
            st = st_ref[hd]
            o = intra[hd] + lax.dot_general(q_t[:, kc], st.astype(BF16), (((1,), (1,)), ((), ())),
                                    preferred_element_type=F32)
            upd = lax.dot_general(v, k_end[:, kc], (((0,), (0,)), ((), ())),
                                  preferred_element_type=F32)
            st_ref[hd] = st * decay[:, kc] + upd
            o = o * lax.rsqrt(jnp.mean(o * o, axis=-1, keepdims=True) + EPS) * nw
            o_ref[rows, vc] = (o * _silu(r_ref[rows, vc])).astype(BF16)
        yield


def _mixer_kernel(sink_ref, h_ref, mod_ref, nw_ref, w_ref, wg_ref, gw_ref, gb_ref, bias_ref, gnw_ref,
                  oa_ref, og_ref, *scratch):
    nb = len(_BUF_FIELDS)
    bufs = (scratch[:nb], scratch[nb:2 * nb])
    tk_ref, tv_ref, st_ref = scratch[2 * nb:]
    s = pl.program_id(1)

    @pl.when(s == 0)
    def _():
        for r in bufs[1] + (tk_ref, tv_ref, st_ref):
            r[...] = jnp.zeros_like(r)

    def step(project_into, mix_from):
        pieces = _project_pieces(h_ref, mod_ref, nw_ref, w_ref, wg_ref, gw_ref, gb_ref, project_into)
        next(pieces)
        gla = _gla_stage(mix_from, gnw_ref, st_ref, og_ref)
        for _ in range(GLA_CHUNKS_BEFORE_ATTN):
            next(gla)
            next(pieces, None)
        for _ in _attention_stage(mix_from, tk_ref, tv_ref, sink_ref, bias_ref, s == 1, oa_ref):
            next(pieces, None)
        for _ in gla:
            next(pieces, None)
        for _ in pieces:
            pass

    @pl.when((s & 1) == 0)
    def _():
        step(bufs[0], bufs[1])

    @pl.when((s & 1) == 1)
    def _():
        step(bufs[1], bufs[0])


def _mixer_core(h3, modl, nw, w_in_b, w_gate_p, gate_w2_p, gate_b, sinks, bias_tab, gnorm, l):
    B, S, D = h3.shape
    nt = S // MIX_TM
    single = pl.Buffered(1)
    proj_tile = lambda b, s: (b, jnp.minimum(s, nt - 1), 0)
    mix_tile = lambda b, s: (b, jnp.maximum(s - 1, 0), 0)
    scratch = [pltpu.VMEM((MIX_TM, w), dt) for _ in range(2) for _, w, dt in _BUF_FIELDS]
    scratch += [pltpu.VMEM((BLOCK, KV_DUP_W), BF16), pltpu.VMEM((BLOCK, KV_DUP_W), BF16),
                pltpu.VMEM((GLA_HEADS, GLA_DV, GLA_DK), F32)]
    return pl.pallas_call(
        _mixer_kernel,
        grid=(B, nt + 1),
        in_specs=[
            pl.BlockSpec(memory_space=pltpu.SMEM),
            pl.BlockSpec((None, MIX_TM, D), proj_tile),
            pl.BlockSpec((None, N_MOD, D), lambda b, s: (b, 0, 0)),
            pl.BlockSpec((None, 1, D), lambda b, s: (l, 0, 0)),
            pl.BlockSpec((None, D, PROJ_W), lambda b, s: (l, 0, 0), pipeline_mode=single),
            pl.BlockSpec((None, D, LANES), lambda b, s: (l, 0, 0), pipeline_mode=single),
            pl.BlockSpec((None, LANES, GK_W), lambda b, s: (l, 0, 0)),
            pl.BlockSpec((None, 1, GK_W), lambda b, s: (l, 0, 0)),
            pl.BlockSpec(bias_tab.shape, lambda b, s: (0, 0, 0), pipeline_mode=single),
            pl.BlockSpec((None, 1, GLA_DV), lambda b, s: (l, 0, 0)),
        ],
        out_specs=[pl.BlockSpec((None, MIX_TM, ATTN_W), mix_tile),
                   pl.BlockSpec((None, MIX_TM, GV_W), mix_tile)],
        out_shape=[jax.ShapeDtypeStruct((B, S, ATTN_W), BF16), jax.ShapeDtypeStruct((B, S, GV_W), BF16)],
        scratch_shapes=scratch,
        compiler_params=_params("parallel", "arbitrary"),
        name="mixer_core",
    )(sinks, h3, modl, nw, w_in_b, w_gate_p, gate_w2_p, gate_b, bias_tab, gnorm)


def _out_kernel(h_ref, mod_ref, oa_ref, og_ref, w_ref, o_ref):
    ka = oa_ref.shape[1]
    y = jnp.dot(oa_ref[...], w_ref[:ka, :], preferred_element_type=F32)
    y = y + jnp.dot(og_ref[...], w_ref[ka:, :], preferred_element_type=F32)
    o_ref[...] = h_ref[...] + mod_ref[5:6, :] * y


def _out_proj(h, modl, oa, og, w_out, l, seq):
    T, D = h.shape
    tpb = seq // OUT_TM
    return pl.pallas_call(
        _out_kernel,
        grid=(T // OUT_TM,),
        in_specs=[
            pl.BlockSpec((OUT_TM, D), lambda i: (i, 0)),
            pl.BlockSpec((None, N_MOD, D), lambda i: (i // tpb, 0, 0)),
            pl.BlockSpec((OUT_TM, oa.shape[1]), lambda i: (i, 0)),
            pl.BlockSpec((OUT_TM, og.shape[1]), lambda i: (i, 0)),
            pl.BlockSpec((None,) + w_out.shape[1:], lambda i: (l, 0, 0)),
        ],
        out_specs=pl.BlockSpec((OUT_TM, D), lambda i: (i, 0)),
        out_shape=jax.ShapeDtypeStruct((T, D), F32),
        compiler_params=_params("parallel"),
        name="mixer_out_proj",
    )(h, modl, oa, og, w_out)


def kernel(x, c, ada_w, ada_b, norm_ffn1, ffn1_w1, ffn1_w3, ffn1_w2, norm_mix, w_in, gla_gate_w2,
           gla_gate_b, attn_sinks, gla_norm, w_out, norm_ffn2, ffn2_w1, ffn2_w3, ffn2_w2,
           final_ada_w, final_ada_b, final_norm):
    B, S, D = x.shape
    L = ada_w.shape[0]
    T = B * S

    c_pad = jnp.pad(c, ((0, SUBLANES - B), (0, 0)))
    mod = _modulation(c_pad, ada_w, ada_b)[:, :B].reshape(L, B, N_MOD, D)
    fmod = _modulation(c_pad, final_ada_w[None], final_ada_b[None])[0, :B].reshape(B, 2, D)

    w11, w13, w12 = _to_bf16(ffn1_w1), _to_bf16(ffn1_w3), _to_bf16(ffn1_w2)
    w21, w23, w22 = _to_bf16(ffn2_w1), _to_bf16(ffn2_w3), _to_bf16(ffn2_w2)
    assert w_in.shape[-1] == PROJ_W
    w_in_b = _to_bf16(w_in)
    gate0 = _SEG["gate"][0]
    w_gate_p = jnp.pad(w_in[:, :, gate0:], ((0, 0), (0, 0), (0, LANES - GLA_RANK))).astype(BF16)
    w_out_b = _to_bf16(w_out)
    gate_w2_p = jnp.pad(gla_gate_w2, ((0, 0), (0, LANES - GLA_RANK), (0, 0))).astype(BF16)
    gate_b = gla_gate_b.reshape(L, 1, GK_W)
    n1, nm, n2 = (t.reshape(L, 1, D) for t in (norm_ffn1, norm_mix, norm_ffn2))
    gnorm = gla_norm.reshape(L, 1, GLA_DV)
    bias_tab = jnp.asarray(_attn_bias_table())

    fnw = final_norm.reshape(1, D)
    h = x.reshape(T, D)
    for l in range(L):
        modl = mod[l]
        h = _ffn(h, _ffn_side(modl, 0, n1[l], fnw, fmod), w11, w13, w12, l, S)
        oa, og = _mixer_core(h.reshape(B, S, D), modl, nm, w_in_b, w_gate_p, gate_w2_p, gate_b,
                             attn_sinks[l], bias_tab, gnorm, l)
        h = _out_proj(h, modl, oa.reshape(T, ATTN_W), og.reshape(T, GV_W), w_out_b, l, S)
        h = _ffn(h, _ffn_side(modl, 6, n2[l], fnw, fmod), w21, w23, w22, l, S, final=(l == L - 1))
    return h.reshape(B, S, D)
```
